```python
import math
import jax, jax.numpy as jnp
from jax import lax
import numpy as np

D_MODEL = 1024
BATCH = 8
SEQ = 4096
DEPTH = 4

GRID_W = 64
CTX_LEN = 256
EPS = 1e-6
H_A = 4
DK_A = 128
DV_A = 128
CONV_K = 5
CHUNK = 64
HQ_B = 8
HKV_B = 2
GQ_B = HQ_B // HKV_B
DH_B = 64
WINDOW = 128
BLK = 128
ROPE_THETA = 10000.0
N_EXPERTS = 16
N_GROUPS = 4
EXP_PER_GROUP = N_EXPERTS // N_GROUPS
GROUP_SCORE_TOPK = 2
TOP_K = 2
D_EXPERT = 512
QKV_A = 2 * H_A * DK_A + H_A * DV_A
IN_SPLITS = (QKV_A, H_A * DV_A, 2 * H_A, 2 * H_A, HQ_B * DH_B, HKV_B * DH_B, HKV_B * DH_B, D_MODEL, D_MODEL)
IN_WIDTH = sum(IN_SPLITS)

kernel_name = "hybrid_gdn_swa_moe_prefix_dit"

f32 = jnp.float32


def rmsnorm(x, g):
    xf = x.astype(f32)
    y = xf * lax.rsqrt(jnp.mean(xf * xf, axis=-1, keepdims=True) + EPS)
    return (y * g.astype(f32)).astype(x.dtype)


def l2norm(x):
    xf = x.astype(f32)
    return xf * lax.rsqrt(jnp.sum(xf * xf, axis=-1, keepdims=True) + EPS)


def adaln_params(cond, w_mod, b_mod):
    m = jax.nn.silu(cond) @ w_mod + b_mod
    return jnp.split(m[..., None, :], 6, axis=-1)


def modulate(h, shift, scale):
    return h * (1 + scale) + shift


def split_in(p):
    return jnp.split(p, [int(i) for i in np.cumsum(IN_SPLITS)[:-1]], axis=-1)


def short_conv_silu(u, w):
    C = u.shape[-1]
    y = lax.conv_general_dilated(u, w[:, None, :].astype(u.dtype), window_strides=(1,),
                                 padding=((CONV_K // 2, CONV_K // 2),),
                                 dimension_numbers=('NWC', 'WIO', 'NWC'), feature_group_count=C)
    return jax.nn.silu(y)


def gdn_chunked(q, k, v, g, beta, s0):
    q, k, v, g, beta = (t.astype(f32) for t in (q, k, v, g, beta))
    B_, L, H, _ = q.shape
    DV = v.shape[-1]
    n = L // CHUNK

    def chunks(t):
        return jnp.swapaxes(t.reshape((B_, n, CHUNK, H) + t.shape[3:]), 2, 3)

    qc, kc, vc, bc = chunks(q), chunks(k), chunks(v), chunks(beta)
    gc = jnp.cumsum(chunks(g), axis=-1)
    idx = jnp.arange(CHUNK)
    tril = idx[:, None] >= idx[None, :]
    strict = idx[:, None] > idx[None, :]
    decay = jnp.where(tril, jnp.exp(jnp.where(tril, gc[..., :, None] - gc[..., None, :], 0.0)), 0.0)
    kb = kc * bc[..., None]
    a_mat = jnp.where(strict, jnp.einsum('bnhid,bnhjd->bnhij', kb, kc) * decay, 0.0)
    lhs = a_mat + jnp.eye(CHUNK, dtype=f32)
    rhs = jnp.concatenate([vc * bc[..., None], kb * jnp.exp(gc)[..., None]], axis=-1)
    sol = lax.linalg.triangular_solve(lhs, rhs, left_side=True, lower=True, unit_diagonal=True)
    u, w = sol[..., :DV], sol[..., DV:]
    attn = jnp.where(tril, jnp.einsum('bnhid,bnhjd->bnhij', qc, kc) * decay, 0.0)
    qg = qc * jnp.exp(gc)[..., None]
    kg = kc * jnp.exp(gc[..., -1:] - gc)[..., None]
    glast = jnp.exp(gc[..., -1])
    xs = tuple(jnp.moveaxis(t, 1, 0) for t in (qg, kg, u, w, attn, glast))

    def step(S, inp):
        qg_, kg_, u_, w_, attn_, gl_ = inp
        v_new = u_ - jnp.einsum('bhck,bhkv->bhcv', w_, S)
        o = jnp.einsum('bhck,bhkv->bhcv', qg_, S) + jnp.einsum('bhij,bhjv->bhiv', attn_, v_new)
        S = S * gl_[..., None, None] + jnp.einsum('bhck,bhcv->bhkv', kg_, v_new)
        return S, o

    s_fin, o = lax.scan(step, s0.astype(f32), xs)
    o = jnp.swapaxes(jnp.moveaxis(o, 0, 1), 2, 3).reshape(B_, L, H, DV)
    return o, s_fin


def mixer_a(qkv, beta_raw, alpha_raw, z, conv_w, a_log, dt_bias, norm_g, s0_f, s0_b):
    B_, L, _ = qkv.shape
    qkv = short_conv_silu(qkv, conv_w)
    q, k, v = jnp.split(qkv, [H_A * DK_A, 2 * H_A * DK_A], axis=-1)
    q = l2norm(q.reshape(B_, L, H_A, DK_A)) * (DK_A ** -0.5)
    k = l2norm(k.reshape(B_, L, H_A, DK_A))
    v = v.reshape(B_, L, H_A, DV_A)
    beta = jax.nn.sigmoid(beta_raw.astype(f32)).reshape(B_, L, 2, H_A)
    g = -jnp.exp(a_log.astype(f32)) * jax.nn.softplus(alpha_raw.astype(f32).reshape(B_, L, 2, H_A) + dt_bias.astype(f32))
    o_f, s_f = gdn_chunked(q, k, v, g[:, :, 0], beta[:, :, 0], s0_f)
    flip = lambda t: jnp.flip(t, axis=1)
    o_b, s_b = gdn_chunked(flip(q), flip(k), flip(v), flip(g[:, :, 1]), flip(beta[:, :, 1]), s0_b)
    o = o_f + flip(o_b)
    o = rmsnorm(o, norm_g) * jax.nn.silu(z.astype(f32).reshape(B_, L, H_A, DV_A))
    return o.reshape(B_, L, H_A * DV_A).astype(qkv.dtype), s_f, s_b


def rope_1d(x, pos):
    nf = x.shape[-1] // 2
    inv = ROPE_THETA ** (-jnp.arange(nf, dtype=f32) / nf)
    ang = pos[:, None] * inv[None, :]
    cos = jnp.cos(ang)[None, :, None, :].astype(x.dtype)
    sin = jnp.sin(ang)[None, :, None, :].astype(x.dtype)
    x1, x2 = x[..., :nf], x[..., nf:]
    return jnp.concatenate([x1 * cos - x2 * sin, x1 * sin + x2 * cos], axis=-1)


def axial_rope(x, rows, cols):
    half = x.shape[-1] // 2
    return jnp.concatenate([rope_1d(x[..., :half], rows), rope_1d(x[..., half:], cols)], axis=-1)


def window_attn(q, k, v, kc, vc, sink):
    B_, L = q.shape[:2]
    Lc = kc.shape[1]
    nb = L // BLK
    nl = 3 * BLK
    qb = q.reshape(B_, nb, BLK, HKV_B, GQ_B, DH_B)

    def band(t):
        tp = jnp.pad(t, ((0, 0), (BLK, BLK), (0, 0), (0, 0))).reshape(B_, nb + 2, BLK, HKV_B, DH_B)
        return jnp.concatenate([tp[:, :-2], tp[:, 1:-1], tp[:, 2:]], axis=2)

    kw, vw = band(k), band(v)
    q_pos = jnp.arange(L).reshape(nb, BLK)
    k_pos = jnp.arange(nb)[:, None] * BLK - BLK + jnp.arange(nl)[None, :]
    kp = k_pos[:, None, :]
    valid = (jnp.abs(q_pos[:, :, None] - kp) <= WINDOW) & (kp >= 0) & (kp < L)
    scale = DH_B ** -0.5
    s_loc = jnp.einsum('bnqkgd,bnpkd->bnkgqp', qb, kw).astype(f32) * scale
    s_loc = jnp.where(valid[None, :, None, None], s_loc, -jnp.inf)
    s_ctx = jnp.einsum('bnqkgd,bckd->bnkgqc', qb, kc).astype(f32) * scale
    s_sink = jnp.broadcast_to(sink.astype(f32).reshape(1, 1, HKV_B, GQ_B, 1, 1), s_loc.shape[:-1] + (1,))
    p = jax.nn.softmax(jnp.concatenate([s_loc, s_ctx, s_sink], axis=-1), axis=-1).astype(v.dtype)
    o = (jnp.einsum('bnkgqp,bnpkd->bnqkgd', p[..., :nl], vw)
         + jnp.einsum('bnkgqc,bckd->bnqkgd', p[..., nl:nl + Lc], vc))
    return o.reshape(B_, L, HQ_B * DH_B)


def context_attn(q, k, v, sink):
    B_, Lc = q.shape[:2]
    qg = q.reshape(B_, Lc, HKV_B, GQ_B, DH_B)
    s = jnp.einsum('bqkgd,bpkd->bkgqp', qg, k).astype(f32) * (DH_B ** -0.5)
    s_sink = jnp.broadcast_to(sink.astype(f32).reshape(1, HKV_B, GQ_B, 1, 1), s.shape[:-1] + (1,))
    p = jax.nn.softmax(jnp.concatenate([s, s_sink], axis=-1), axis=-1)[..., :Lc].astype(v.dtype)
    o = jnp.einsum('bkgqp,bpkd->bqkgd', p, v)
    return o.reshape(B_, Lc, HQ_B * DH_B)


def merge_branches(ya, yb, gate_a, gate_b, w_oa, w_ob, w_o):
    return (jax.nn.sigmoid(gate_a) * (ya @ w_oa) + jax.nn.sigmoid(gate_b) * (yb @ w_ob)) @ w_o


def mix_tokens(h, hc, rows, cols, w_in, conv_w, a_log, dt_bias, gdn_g, sink, w_oa, w_ob, w_o, need_ctx):
    B_ = h.shape[0]
    qkv_a, z_a, b_a, a_a, q_b, k_b, v_b, gate_a, gate_b = split_in(h @ w_in)
    qkv_ac, z_ac, b_ac, a_ac, q_bc, k_bc, v_bc, gate_ac, gate_bc = split_in(hc @ w_in)
    s0 = jnp.zeros((B_, H_A, DK_A, DV_A), f32)
    ya_c, s_f, s_b = mixer_a(qkv_ac, b_ac, a_ac, z_ac, conv_w, a_log, dt_bias, gdn_g, s0, s0)
    ya, _, _ = mixer_a(qkv_a, b_a, a_a, z_a, conv_w, a_log, dt_bias, gdn_g, s_f, s_b)
    heads = lambda t, nh: t.reshape(t.shape[0], t.shape[1], nh, DH_B)
    q = axial_rope(heads(q_b, HQ_B), rows, cols)
    k = axial_rope(heads(k_b, HKV_B), rows, cols)
    kc, vc = heads(k_bc, HKV_B), heads(v_bc, HKV_B)
    yb = window_attn(q, k, heads(v_b, HKV_B), kc, vc, sink)
    y = merge_branches(ya, yb, gate_a, gate_b, w_oa, w_ob, w_o)
    if not need_ctx:
        return y, None
    yb_c = context_attn(heads(q_bc, HQ_B), kc, vc, sink)
    return y, merge_branches(ya_c, yb_c, gate_ac, gate_bc, w_oa, w_ob, w_o)


def moe(h, w_router, router_bias, w_gate, w_up, w_down):
    B_, L, D = h.shape
    t = h.reshape(B_ * L, D)
    scores = jax.nn.sigmoid((t @ w_router).astype(f32))
    sel = scores + router_bias.astype(f32)
    group_score = lax.top_k(sel.reshape(-1, N_GROUPS, EXP_PER_GROUP), GROUP_SCORE_TOPK)[0].sum(-1)
    g_idx = jnp.argmax(group_score, axis=-1)
    in_group = (jnp.arange(N_EXPERTS) // EXP_PER_GROUP)[None, :] == g_idx[:, None]
    _, e_idx = lax.top_k(jnp.where(in_group, sel, -jnp.inf), TOP_K)
    wts = jnp.take_along_axis(scores, e_idx, axis=-1)
    wts = wts / jnp.sum(wts, axis=-1, keepdims=True)
    combine = jnp.sum(jax.nn.one_hot(e_idx, N_EXPERTS, dtype=f32) * wts[..., None], axis=1).astype(h.dtype)
    out = jnp.zeros_like(t)
    for e in range(N_EXPERTS):
        he = jax.nn.silu(t @ w_gate[e]) * (t @ w_up[e])
        out = out + combine[:, e:e + 1] * (he @ w_down[e])
    return out.reshape(B_, L, D)


def setup_inputs(seed: int = 0) -> dict:
    key = jax.random.key(seed)
    ks = jax.random.split(key, 24)

    def nrm(k, shape, scale):
        return jax.random.normal(k, shape, f32) * scale

    dt = jnp.exp(jax.random.uniform(ks[11], (DEPTH, 2, H_A), f32, math.log(1e-3), math.log(1e-1)))
    return {
        "x": nrm(ks[0], (BATCH, SEQ, D_MODEL), 1.0),
        "c": nrm(ks[1], (BATCH, D_MODEL), 1.0),
        "ctx": nrm(ks[2], (BATCH, CTX_LEN, D_MODEL), 1.0),
        "c_ctx": nrm(ks[3], (D_MODEL,), 1.0),
        "w_mod": nrm(ks[4], (DEPTH, D_MODEL, 6 * D_MODEL), 0.5 * D_MODEL ** -0.5),
        "b_mod": nrm(ks[5], (DEPTH, 6 * D_MODEL), 0.02),
        "norm1_g": 1.0 + nrm(ks[6], (DEPTH, D_MODEL), 0.02),
        "norm2_g": 1.0 + nrm(ks[7], (DEPTH, D_MODEL), 0.02),
        "w_in": nrm(ks[8], (DEPTH, D_MODEL, IN_WIDTH), D_MODEL ** -0.5),
        "conv_w": nrm(ks[9], (DEPTH, CONV_K, QKV_A), CONV_K ** -0.5),
        "a_log": jnp.log(jax.random.uniform(ks[10], (DEPTH, 2, H_A), f32, 1.0, 16.0)),
        "dt_bias": dt + jnp.log(-jnp.expm1(-dt)),
        "gdn_norm_g": 1.0 + nrm(ks[12], (DEPTH, DV_A), 0.02),
        "sink": nrm(ks[13], (DEPTH, HQ_B), 0.5),
        "w_oa": nrm(ks[14], (DEPTH, H_A * DV_A, D_MODEL), (H_A * DV_A) ** -0.5),
        "w_ob": nrm(ks[15], (DEPTH, HQ_B * DH_B, D_MODEL), (HQ_B * DH_B) ** -0.5),
        "w_o": nrm(ks[16], (DEPTH, D_MODEL, D_MODEL), D_MODEL ** -0.5),
        "w_router": nrm(ks[17], (D_MODEL, N_EXPERTS), D_MODEL ** -0.5),
        "router_bias": nrm(ks[18], (N_EXPERTS,), 0.01),
        "w_gate": nrm(ks[19], (DEPTH, N_EXPERTS, D_MODEL, D_EXPERT), D_MODEL ** -0.5),
        "w_up": nrm(ks[20], (DEPTH, N_EXPERTS, D_MODEL, D_EXPERT), D_MODEL ** -0.5),
        "w_down": nrm(ks[21], (DEPTH, N_EXPERTS, D_EXPERT, D_MODEL), D_EXPERT ** -0.5),
        "final_g": 1.0 + nrm(ks[22], (D_MODEL,), 0.02),
    }


def reference(x, c, ctx, c_ctx, w_mod, b_mod, norm1_g, norm2_g, w_in, conv_w, a_log, dt_bias,
              gdn_norm_g, sink, w_oa, w_ob, w_o, w_router, router_bias, w_gate, w_up, w_down, final_g):
    L = x.shape[1]
    n_rows = L // GRID_W
    rows = jnp.repeat(jnp.arange(n_rows, dtype=f32), GRID_W)
    cols = jnp.tile(jnp.arange(GRID_W, dtype=f32), n_rows)
    xc = ctx
    for l in range(DEPTH):
        need_ctx = l < DEPTH - 1
        sh1, sc1, g1, sh2, sc2, g2 = adaln_params(c, w_mod[l], b_mod[l])
        csh1, csc1, cg1, csh2, csc2, cg2 = adaln_params(c_ctx, w_mod[l], b_mod[l])
        h = modulate(rmsnorm(x, norm1_g[l]), sh1, sc1)
        hc = modulate(rmsnorm(xc, norm1_g[l]), csh1, csc1)
        y, yc = mix_tokens(h, hc, rows, cols, w_in[l], conv_w[l], a_log[l], dt_bias[l], gdn_norm_g[l],
                           sink[l], w_oa[l], w_ob[l], w_o[l], need_ctx)
        x = x + g1 * y
        h = modulate(rmsnorm(x, norm2_g[l]), sh2, sc2)
        x = x + g2 * moe(h, w_router, router_bias, w_gate[l], w_up[l], w_down[l])
        if need_ctx:
            xc = xc + cg1 * yc
            hc = modulate(rmsnorm(xc, norm2_g[l]), csh2, csc2)
            xc = xc + cg2 * moe(hc, w_router, router_bias, w_gate[l], w_up[l], w_down[l])
    return rmsnorm(x, final_g)
```

```python
import functools
import math

import jax
import jax.numpy as jnp
import numpy as np
from jax import lax
from jax.experimental import pallas as pl
from jax.experimental.pallas import tpu as pltpu

f32 = jnp.float32
bf16 = jnp.bfloat16

D_MODEL = 1024
GRID_W = 64
EPS = 1e-6
H_A = 4
DK_A = 128
DV_A = 128
CONV_K = 5
CHUNK = 64
HQ_B = 8
HKV_B = 2
DH_B = 64
WINDOW = 128
BLK = 128
ROPE_THETA = 10000.0
N_EXPERTS = 16
N_GROUPS = 4
EXP_PER_GROUP = 4
D_EXPERT = 512

LANES = 128
SUBLANES = 8
VMEM_BYTES_V7X = 64 * 1024 * 1024

W_QKV = 3 * H_A * DK_A
W_Z = H_A * DV_A
W_BA = LANES
W_QB = HQ_B * DH_B
W_KV = 2 * HKV_B * DH_B
W_GATE = 2 * D_MODEL
OFF_Z = W_QKV
OFF_BA = OFF_Z + W_Z
OFF_QB = OFF_BA + W_BA
OFF_KV = OFF_QB + W_QB
OFF_GATE = OFF_KV + W_KV
W_TOTAL = OFF_GATE + W_GATE
W_ROPE = W_QB + HKV_B * DH_B

NEG_BIG = -1e30
HEAD_PERM = (0, 4, 1, 5, 2, 6, 3, 7)

SUPER = 256


def _vmem_limit(nbytes):
    return int(min(max(nbytes, 16 * 1024 * 1024), VMEM_BYTES_V7X - 8 * 1024 * 1024))


def _split3(x):
    hi = x.astype(bf16)
    r1 = x - hi.astype(f32)
    mid = r1.astype(bf16)
    lo = (r1 - mid.astype(f32)).astype(bf16)
    return hi, mid, lo


def _dot(a, b):
    return jnp.dot(a, b, preferred_element_type=f32)


def _dot_nt(a, b):
    return lax.dot_general(a, b, (((1,), (1,)), ((), ())), preferred_element_type=f32)


def _dot_exact_lhs_mask(mask_bf, parts):
    return _dot(mask_bf, parts[0]) + _dot(mask_bf, parts[1]) + _dot(mask_bf, parts[2])


def _sigmoid(x):
    return 1.0 / (1.0 + jnp.exp(-x))


def _silu(x):
    return x * _sigmoid(x)


def _softplus(x):
    return jnp.maximum(x, 0.0) + jnp.log1p(jnp.exp(-jnp.abs(x)))


def _adaln_kernel(c_ref, w_ref, b_ref, o_ref):
    c = c_ref[...]
    s = _silu(c).astype(bf16)
    o_ref[...] = _dot(s, w_ref[...].astype(bf16)) + b_ref[...]


def adaln_all(cond, w_mod, b_mod):
    depth = w_mod.shape[0]
    r = cond.shape[0]
    nblk = 6
    return pl.pallas_call(
        _adaln_kernel,
        out_shape=jax.ShapeDtypeStruct((depth, r, 6 * D_MODEL), f32),
        grid=(depth, nblk),
        in_specs=[
            pl.BlockSpec((r, D_MODEL), lambda l, j: (0, 0)),
            pl.BlockSpec((None, D_MODEL, D_MODEL), lambda l, j: (l, 0, j)),
            pl.BlockSpec((None, 1, D_MODEL), lambda l, j: (l, 0, j)),
        ],
        out_specs=pl.BlockSpec((None, r, D_MODEL), lambda l, j: (l, 0, j)),
        compiler_params=pltpu.CompilerParams(
            dimension_semantics=("parallel", "parallel"),
            vmem_limit_bytes=_vmem_limit(24 * 1024 * 1024)),
        name="adaln",
    )(cond, w_mod, b_mod.reshape(depth, 1, 6 * D_MODEL))


def _inproj_kernel(x_ref, g_ref, sh_ref, sc_ref, cos_ref, sa_ref, sb_ref, w_ref,
                   qkv_ref, z_ref, ba_ref, qb_ref, kv_ref, gate_ref):
    x = x_ref[...]
    ms = jnp.mean(x * x, axis=-1, keepdims=True)
    y = x * lax.rsqrt(ms + EPS) * g_ref[...]
    h = (y * (1.0 + sc_ref[...]) + sh_ref[...]).astype(bf16)

    qkv_ref[...] = _dot(h, w_ref[:, 0:W_QKV]).astype(bf16)
    z_ref[...] = _dot(h, w_ref[:, OFF_Z:OFF_Z + W_Z]).astype(bf16)
    ba_ref[...] = _dot(h, w_ref[:, OFF_BA:OFF_BA + W_BA])

    pr = _dot(h, w_ref[:, OFF_QB:OFF_QB + W_ROPE])
    reps = W_ROPE // LANES
    cos = jnp.concatenate([cos_ref[...]] * reps, axis=1)
    sa = jnp.concatenate([sa_ref[...]] * reps, axis=1)
    sb = jnp.concatenate([sb_ref[...]] * reps, axis=1)
    half = DH_B // 4
    rot = pr * cos + pltpu.roll(pr, W_ROPE - half, 1) * sa + pltpu.roll(pr, half, 1) * sb
    qb_ref[...] = rot[:, 0:W_QB].astype(bf16)
    kv_ref[:, 0:LANES] = rot[:, W_QB:W_ROPE].astype(bf16)
    kv_ref[:, LANES:2 * LANES] = _dot(h, w_ref[:, OFF_KV + LANES:OFF_KV + 2 * LANES]).astype(bf16)

    gate_ref[...] = _sigmoid(_dot(h, w_ref[:, OFF_GATE:OFF_GATE + W_GATE])).astype(bf16)


def inproj(x_all, mod_l, norm_g, w_cat, rope_tabs, *, n_lat_tiles_per_seq, n_lat_tiles, batch, tm):
    t_all = x_all.shape[0]
    n_tiles = t_all // tm
    cos_t, sa_t, sb_t = rope_tabs

    def seg(i):
        return jnp.where(i < n_lat_tiles, i // n_lat_tiles_per_seq, batch)

    def pos(i):
        return jnp.where(i < n_lat_tiles, i % n_lat_tiles_per_seq, n_lat_tiles_per_seq)

    row = lambda w: pl.BlockSpec((tm, w), lambda i: (i, 0))
    tab = pl.BlockSpec((tm, LANES), lambda i: (pos(i), 0))
    out_shapes = (
        jax.ShapeDtypeStruct((t_all, W_QKV), bf16),
        jax.ShapeDtypeStruct((t_all, W_Z), bf16),
        jax.ShapeDtypeStruct((t_all, W_BA), f32),
        jax.ShapeDtypeStruct((t_all, W_QB), bf16),
        jax.ShapeDtypeStruct((t_all, W_KV), bf16),
        jax.ShapeDtypeStruct((t_all, W_GATE), bf16),
    )
    vmem = 2 * (W_TOTAL * D_MODEL * 2) + 2 * tm * (D_MODEL * 4 + W_TOTAL * 2 + 3 * LANES * 4) + tm * W_TOTAL * 4
    return pl.pallas_call(
        _inproj_kernel,
        out_shape=out_shapes,
        grid=(n_tiles,),
        in_specs=[
            row(D_MODEL),
            pl.BlockSpec((1, D_MODEL), lambda i: (0, 0)),
            pl.BlockSpec((None, 1, D_MODEL), lambda i: (seg(i), 0, 0)),
            pl.BlockSpec((None, 1, D_MODEL), lambda i: (seg(i), 0, 1)),
            tab, tab, tab,
            pl.BlockSpec((D_MODEL, W_TOTAL), lambda i: (0, 0)),
        ],
        out_specs=(row(W_QKV), row(W_Z), row(W_BA), row(W_QB), row(W_KV), row(W_GATE)),
        compiler_params=pltpu.CompilerParams(
            dimension_semantics=("parallel",), vmem_limit_bytes=_vmem_limit(vmem)),
        name="inproj",
    )(x_all, norm_g.reshape(1, D_MODEL), mod_l, mod_l, cos_t, sa_t, sb_t, w_cat)


def _gdn_kernel(alog_ref, dtb_ref,
                q_ref, k_ref, v_ref, ba_ref, z_ref, cwq_ref, cwk_ref, cwv_ref, gn_ref, s0f_ref, s0b_ref,
                yin_ref, y_ref, sf_ref, sb_ref,
                u_s, w_s, qg_s, kg_s, at_s, gl_s, o_s, st_s, *, seq, nsuper):
    del yin_ref
    hd = pl.program_id(1)
    nch = seq // CHUNK
    cps = SUPER // CHUNK

    ii = lax.broadcasted_iota(jnp.int32, (SUPER, SUPER), 0)
    jj = lax.broadcasted_iota(jnp.int32, (SUPER, SUPER), 1)
    same = (ii // CHUNK) == (jj // CHUNK)
    low = same & (jj <= ii)
    upp = same & (jj >= ii)
    low_s = same & (jj < ii)
    upp_s = same & (jj > ii)
    blk8 = (ii // SUBLANES) == (jj // SUBLANES)
    eye = (ii == jj).astype(f32)
    same_bf = jnp.where(same, 1.0, 0.0).astype(bf16)
    low_bf = jnp.where(low, 1.0, 0.0).astype(bf16)
    upp_bf = jnp.where(upp, 1.0, 0.0).astype(bf16)

    kk = lax.broadcasted_iota(jnp.int32, (LANES, 4 * LANES), 0)
    cc = lax.broadcasted_iota(jnp.int32, (LANES, 4 * LANES), 1) // LANES
    sel_bf = jnp.where(kk == cc * H_A + hd, 1.0, 0.0).astype(bf16)

    def mm(a, b):
        return _dot(a.astype(bf16), b.astype(bf16))

    def conv_silu(ref, cw_ref, base, st):
        main = ref[pl.ds(base, SUPER), :].astype(f32)
        p0 = pl.multiple_of(jnp.maximum(base - 16, 0), 16)
        n0 = pl.multiple_of(jnp.minimum(base + SUPER, seq - 16), 16)
        prev = ref[pl.ds(p0, 16), :].astype(f32) * (st > 0).astype(f32)
        nxt = ref[pl.ds(n0, 16), :].astype(f32) * (st < nsuper - 1).astype(f32)
        win = jnp.concatenate([prev, main, nxt], axis=0)
        acc = jnp.zeros((SUPER, LANES), f32)
        for i in range(CONV_K):
            lo_row = 16 - CONV_K // 2 + i
            acc = acc + win[lo_row:lo_row + SUPER, :] * cw_ref[i:i + 1, :]
        return _silu(acc)

    def precompute(st, carry):
        base = pl.multiple_of(st * SUPER, SUPER)
        q = conv_silu(q_ref, cwq_ref, base, st)
        k = conv_silu(k_ref, cwk_ref, base, st)
        v = conv_silu(v_ref, cwv_ref, base, st)
        q = q * lax.rsqrt(jnp.sum(q * q, axis=-1, keepdims=True) + EPS) * (DK_A ** -0.5)
        k = k * lax.rsqrt(jnp.sum(k * k, axis=-1, keepdims=True) + EPS)
        q_bf = q.astype(bf16)
        k_bf = k.astype(bf16)
        kk_raw = _dot_nt(k_bf, k_bf)
        qk_raw = _dot_nt(q_bf, k_bf)

        ba = ba_ref[pl.ds(base, SUPER), :]
        bsel = _dot_exact_lhs_mask_rhs(ba, sel_bf)

        for d in range(2):
            beta = _sigmoid(bsel[:, d * LANES:(d + 1) * LANES])
            alpha = bsel[:, (2 + d) * LANES:(3 + d) * LANES]
            a_dec = jnp.exp(jnp.full((1, LANES), alog_ref[d, hd], f32))
            g = -a_dec * _softplus(alpha + dtb_ref[d, hd])
            g3 = _split3(g)
            cum_bf, cumt = (low_bf, upp) if d == 0 else (upp_bf, low)
            mask, mask_s = (low, low_s) if d == 0 else (upp, upp_s)
            c128 = _dot_exact_lhs_mask(cum_bf, g3)
            tot128 = _dot_exact_lhs_mask(same_bf, g3)
            g256 = jnp.concatenate([g, g], axis=1)
            gu3 = _split3(jnp.where(cumt, g256, 0.0))
            r256 = _dot_exact_lhs_mask(same_bf, gu3)
            c256 = jnp.concatenate([c128, c128], axis=1)
            decay = jnp.where(mask, jnp.exp(jnp.where(mask, c256 - r256, 0.0)), 0.0)
            beta256 = jnp.concatenate([beta, beta], axis=1)
            a_mat = jnp.where(mask_s, beta256 * kk_raw * decay, 0.0)
            attn = jnp.where(mask, qk_raw * decay, 0.0)

            dm = jnp.where(blk8, a_mat, 0.0)
            nm = a_mat - dm
            d2 = mm(dm, dm)
            d3 = mm(dm, d2)
            x1m = d2 - dm - d3
            d4 = mm(d2, d2)
            tdm = x1m + d4 + mm(x1m, d4)
            m1 = nm + mm(tdm, nm)
            m2 = mm(m1, m1)
            m3 = mm(m1, m2)
            y1m = m2 - m1 - m3
            m4 = mm(m2, m2)
            ym = y1m + m4 + mm(y1m, m4)
            tm_ = tdm + ym + mm(ym, tdm)

            egc = jnp.exp(c128)
            kb = k * beta
            rhs = jnp.concatenate([v * beta, kb * egc], axis=1)
            sol = rhs + mm(tm_, rhs)
            rows = pl.ds(base, SUPER)
            u_s[d, rows, :] = sol[:, 0:LANES].astype(bf16)
            w_s[d, rows, :] = sol[:, LANES:2 * LANES].astype(bf16)
            qg_s[d, rows, :] = (q * egc).astype(bf16)
            kg_s[d, rows, :] = (k * jnp.exp(tot128 - c128)).astype(bf16)
            glast = jnp.exp(tot128)
            for c in range(cps):
                at_s[d, pl.ds(base + c * CHUNK, CHUNK), :] = (
                    attn[c * CHUNK:(c + 1) * CHUNK, c * CHUNK:(c + 1) * CHUNK].astype(bf16))
                gl_s[d, pl.ds(st * cps + c, 1), :] = glast[c * CHUNK:c * CHUNK + 1, :]
        return carry

    lax.fori_loop(0, nsuper, precompute, 0)

    st_s[0] = s0f_ref[...]
    st_s[1] = s0b_ref[...]

    def scan_step(n, accumulate):
        for d in range(2):
            c = n if d == 0 else nch - 1 - n
            r0 = pl.multiple_of(c * CHUNK, CHUNK)
            rows = pl.ds(r0, CHUNK)
            u = u_s[d, rows, :].astype(f32)
            w = w_s[d, rows, :]
            qg = qg_s[d, rows, :]
            kg = kg_s[d, rows, :]
            attn = at_s[d, rows, :]
            s_old = st_s[d]
            ws = _dot(jnp.concatenate([w, qg], axis=0), s_old.astype(bf16))
            v_new = (u - ws[0:CHUNK]).astype(bf16)
            o = ws[CHUNK:2 * CHUNK] + _dot(attn, v_new)
            gl = gl_s[d, pl.ds(c, 1), :]
            upd = lax.dot_general(kg, v_new, (((0,), (0,)), ((), ())), preferred_element_type=f32)
            st_s[d] = s_old * gl + upd
            if accumulate:
                o_s[rows, :] = o_s[rows, :] + o
            else:
                o_s[rows, :] = o

    def scan_a(n, carry):
        scan_step(n, False)
        return carry

    def scan_b(n, carry):
        scan_step(n, True)
        return carry

    lax.fori_loop(0, nch // 2, scan_a, 0)
    lax.fori_loop(nch // 2, nch, scan_b, 0)

    def finish(st, carry):
        rows = pl.ds(pl.multiple_of(st * SUPER, SUPER), SUPER)
        o = o_s[rows, :]
        yn = o * lax.rsqrt(jnp.mean(o * o, axis=-1, keepdims=True) + EPS) * gn_ref[...]
        y_ref[rows, :] = (yn * _silu(z_ref[rows, :].astype(f32))).astype(bf16)
        return carry

    lax.fori_loop(0, nsuper, finish, 0)
    sf_ref[...] = st_s[0]
    sb_ref[...] = st_s[1]


def _dot_exact_lhs_mask_rhs(x, sel_bf):
    hi, mid, lo = _split3(x)
    return _dot(hi, sel_bf) + _dot(mid, sel_bf) + _dot(lo, sel_bf)


def gdn(qkv_all, ba_all, z_all, ya_all, conv_w, a_log, dt_bias, gdn_g, s0f, s0b, *, batch, seq, row_block0):
    nsuper = seq // SUPER
    nch = seq // CHUNK
    nch_pad = max(nch, SUBLANES)
    cw = jnp.zeros((3 * H_A, SUBLANES, LANES), f32).at[:, :CONV_K, :].set(
        conv_w.reshape(CONV_K, 3 * H_A, LANES).transpose(1, 0, 2))

    col = lambda off: pl.BlockSpec((seq, LANES), lambda b, h: (row_block0 + b, off + h))
    cwspec = lambda off: pl.BlockSpec((None, SUBLANES, LANES), lambda b, h: (off + h, 0, 0))
    st_spec = pl.BlockSpec((None, None, DK_A, DV_A), lambda b, h: (b, h, 0, 0))
    smem = pl.BlockSpec(memory_space=pltpu.SMEM)
    kern = functools.partial(_gdn_kernel, seq=seq, nsuper=nsuper)
    vmem = (2 * seq * LANES * (3 * 2 + 2 + 4 + 2) + seq * LANES * (8 * 2 + 2 * 2 + 4)
            + 4 * 4 * DK_A * DV_A * 4 + 24 * 1024 * 1024)
    return pl.pallas_call(
        kern,
        out_shape=(
            jax.ShapeDtypeStruct(ya_all.shape, bf16),
            jax.ShapeDtypeStruct((batch, H_A, DK_A, DV_A), f32),
            jax.ShapeDtypeStruct((batch, H_A, DK_A, DV_A), f32),
        ),
        grid=(batch, H_A),
        in_specs=[
            smem, smem,
            col(0), col(H_A), col(2 * H_A),
            pl.BlockSpec((seq, LANES), lambda b, h: (row_block0 + b, 0)),
            col(0),
            cwspec(0), cwspec(H_A), cwspec(2 * H_A),
            pl.BlockSpec((1, DV_A), lambda b, h: (0, 0)),
            st_spec, st_spec,
            pl.BlockSpec(memory_space=pl.ANY),
        ],
        out_specs=(col(0), st_spec, st_spec),
        scratch_shapes=[
            pltpu.VMEM((2, seq, LANES), bf16),
            pltpu.VMEM((2, seq, LANES), bf16),
            pltpu.VMEM((2, seq, LANES), bf16),
            pltpu.VMEM((2, seq, LANES), bf16),
            pltpu.VMEM((2, seq, CHUNK), bf16),
            pltpu.VMEM((2, nch_pad, LANES), f32),
            pltpu.VMEM((seq, LANES), f32),
            pltpu.VMEM((2, DK_A, DV_A), f32),
        ],
        input_output_aliases={13: 0},
        compiler_params=pltpu.CompilerParams(
            dimension_semantics=("parallel", "parallel"), vmem_limit_bytes=_vmem_limit(vmem)),
        name="gdn",
    )(a_log, dt_bias, qkv_all, qkv_all, qkv_all, ba_all, z_all, cw, cw, cw,
      gdn_g.reshape(1, DV_A), s0f, s0b, ya_all)


def _attn_kernel(sink_ref, q_ref, kp_ref, kc_ref, kn_ref, kx_ref, o_ref, *, n_lat_blocks, blocks_per_seq):
    i = pl.program_id(0)
    is_lat = i < n_lat_blocks
    n = i % blocks_per_seq
    kv = jnp.concatenate([kp_ref[...], kc_ref[...], kn_ref[...], kx_ref[...]], axis=0)
    k2 = kv[:, 0:LANES]
    v2 = kv[:, LANES:2 * LANES]
    nctx = kx_ref.shape[0]

    row = lax.broadcasted_iota(jnp.int32, (BLK, BLK), 0)
    col = lax.broadcasted_iota(jnp.int32, (BLK, BLK), 1)
    ok_prev = (col >= row) & is_lat & (n > 0)
    ok_cent = jnp.broadcast_to(is_lat, (BLK, BLK))
    ok_next = (col <= row) & is_lat & (n < blocks_per_seq - 1)
    zeros = jnp.zeros((BLK, BLK), f32)
    bias = jnp.concatenate(
        [jnp.where(ok_prev, zeros, NEG_BIG), jnp.where(ok_cent, zeros, NEG_BIG),
         jnp.where(ok_next, zeros, NEG_BIG), jnp.zeros((BLK, nctx), f32)], axis=1)
    bias2 = jnp.concatenate([bias, bias], axis=0)

    lane = lax.broadcasted_iota(jnp.int32, (BLK, LANES), 1)
    lo = lane < DH_B
    lo_bf = jnp.where(lo, 1.0, 0.0).astype(bf16)
    hi_bf = jnp.where(lo, 0.0, 1.0).astype(bf16)
    rows2 = lax.broadcasted_iota(jnp.int32, (2 * BLK, 1), 0)
    for p in range(HQ_B // 2):
        q2 = q_ref[:, p * LANES:(p + 1) * LANES]
        qs = jnp.concatenate([q2 * lo_bf, q2 * hi_bf], axis=0)
        s = _dot_nt(qs, k2) + bias2
        sink = jnp.where(rows2 < BLK, sink_ref[p], sink_ref[HQ_B // 2 + p])
        m = jnp.maximum(jnp.max(s, axis=-1, keepdims=True), sink)
        e = jnp.exp(s - m)
        den = jnp.sum(e, axis=-1, keepdims=True) + jnp.exp(sink - m)
        o = _dot(e.astype(bf16), v2) * (1.0 / den)
        o_ref[:, p * LANES:(p + 1) * LANES] = jnp.where(lo, o[0:BLK], o[BLK:2 * BLK]).astype(bf16)


def attention(qb_all, kv_all, sink, *, batch, seq, ctx_len, with_ctx_queries):
    t_lat = batch * seq
    bps = seq // BLK
    n_lat = t_lat // BLK
    cps = ctx_len // BLK
    n_blocks = n_lat + (batch * cps if with_ctx_queries else 0)
    ctx_blk0 = t_lat // ctx_len

    def bidx(i):
        return jnp.where(i < n_lat, i // bps, (i - n_lat) // cps)

    def nbr(delta):
        def f(i):
            n = i % bps
            j = jnp.clip(n + delta, 0, bps - 1)
            return jnp.where(i < n_lat, (i // bps) * bps + j, 0)
        return f

    kern = functools.partial(_attn_kernel, n_lat_blocks=n_lat, blocks_per_seq=bps)
    return pl.pallas_call(
        kern,
        out_shape=jax.ShapeDtypeStruct(qb_all.shape, bf16),
        grid=(n_blocks,),
        in_specs=[
            pl.BlockSpec(memory_space=pltpu.SMEM),
            pl.BlockSpec((BLK, W_QB), lambda i: (i, 0)),
            pl.BlockSpec((BLK, W_KV), lambda i: (nbr(-1)(i), 0)),
            pl.BlockSpec((BLK, W_KV), lambda i: (nbr(0)(i), 0)),
            pl.BlockSpec((BLK, W_KV), lambda i: (nbr(1)(i), 0)),
            pl.BlockSpec((ctx_len, W_KV), lambda i: (ctx_blk0 + bidx(i), 0)),
        ],
        out_specs=pl.BlockSpec((BLK, W_QB), lambda i: (i, 0)),
        compiler_params=pltpu.CompilerParams(
            dimension_semantics=("parallel",), vmem_limit_bytes=_vmem_limit(32 * 1024 * 1024)),
        name="attn",
    )(sink, qb_all, kv_all, kv_all, kv_all, kv_all)


def _merge_kernel(x_ref, ya_ref, yb_ref, gate_ref, woa_ref, wob_ref, wo_ref, g1_ref, n2_ref, sh2_ref, sc2_ref,
                  wrh_ref, wrl_ref, rb_ref, x1_ref, h2_ref, comb_ref):
    pa = _dot(ya_ref[...], woa_ref[...])
    pb = _dot(yb_ref[...], wob_ref[...])
    m = gate_ref[:, 0:D_MODEL].astype(f32) * pa + gate_ref[:, D_MODEL:2 * D_MODEL].astype(f32) * pb
    y = _dot(m.astype(bf16), wo_ref[...])
    x1 = x_ref[...] + g1_ref[...] * y
    x1_ref[...] = x1
    ms = jnp.mean(x1 * x1, axis=-1, keepdims=True)
    h2 = (x1 * lax.rsqrt(ms + EPS) * n2_ref[...]) * (1.0 + sc2_ref[...]) + sh2_ref[...]
    h_hi = h2.astype(bf16)
    h2_ref[...] = h_hi
    h_lo = (h2 - h_hi.astype(f32)).astype(bf16)

    logit = _dot_nt(wrh_ref[...], h_hi) + _dot_nt(wrl_ref[...], h_hi) + _dot_nt(wrh_ref[...], h_lo)
    score = _sigmoid(logit)
    sel = score + rb_ref[...]
    tm = sel.shape[1]

    def top2_in_group(g):
        r = [sel[g * EXP_PER_GROUP + j:g * EXP_PER_GROUP + j + 1, :] for j in range(EXP_PER_GROUP)]
        pair = None
        for a in range(EXP_PER_GROUP):
            for b in range(a + 1, EXP_PER_GROUP):
                s_ab = r[a] + r[b]
                pair = s_ab if pair is None else jnp.maximum(pair, s_ab)
        best = r[0]
        i1 = jnp.zeros((1, tm), jnp.int32)
        for j in range(1, EXP_PER_GROUP):
            better = r[j] > best
            best = jnp.where(better, r[j], best)
            i1 = jnp.where(better, j, i1)
        second = jnp.full((1, tm), -jnp.inf, f32)
        i2 = jnp.zeros((1, tm), jnp.int32)
        for j in range(EXP_PER_GROUP):
            cand = jnp.where(i1 == j, -jnp.inf, r[j])
            better = cand > second
            second = jnp.where(better, cand, second)
            i2 = jnp.where(better, j, i2)
        return pair, i1 + g * EXP_PER_GROUP, i2 + g * EXP_PER_GROUP

    gs, e1, e2 = top2_in_group(0)
    for g in range(1, N_GROUPS):
        gs_g, e1_g, e2_g = top2_in_group(g)
        better = gs_g > gs
        gs = jnp.where(better, gs_g, gs)
        e1 = jnp.where(better, e1_g, e1)
        e2 = jnp.where(better, e2_g, e2)

    eidx = lax.broadcasted_iota(jnp.int32, (N_EXPERTS, tm), 0)
    hit1 = eidx == e1
    hit2 = eidx == e2
    s1 = jnp.sum(jnp.where(hit1, score, 0.0), axis=0, keepdims=True)
    s2 = jnp.sum(jnp.where(hit2, score, 0.0), axis=0, keepdims=True)
    tot = s1 + s2
    comb_t = jnp.where(hit1, s1 / tot, 0.0) + jnp.where(hit2, s2 / tot, 0.0)
    comb_pad = jnp.concatenate([comb_t, jnp.zeros((LANES - N_EXPERTS, tm), f32)], axis=0)
    comb_ref[...] = comb_pad.T


def merge(x_all, ya_all, yb_all, gate_all, w_oa, w_ob, w_o, mod_l, norm2_g, wr_hi, wr_lo, router_bias,
          *, n_rows, n_lat_tiles_per_seq, n_lat_tiles, batch, tm):
    n_tiles = n_rows // tm

    def seg(i):
        return jnp.where(i < n_lat_tiles, i // n_lat_tiles_per_seq, batch)

    row = lambda w: pl.BlockSpec((tm, w), lambda i: (i, 0))
    full = lambda a, b: pl.BlockSpec((a, b), lambda i: (0, 0))
    modspec = lambda j: pl.BlockSpec((None, 1, D_MODEL), lambda i: (seg(i), 0, j))
    vmem = 2 * (2 * W_Z * D_MODEL * 2 + D_MODEL * D_MODEL * 2) + 2 * tm * (
        D_MODEL * 4 * 2 + 2 * W_Z * 2 + W_GATE * 2 + D_MODEL * 2 + LANES * 4) + 8 * tm * D_MODEL * 4
    return pl.pallas_call(
        _merge_kernel,
        out_shape=(
            jax.ShapeDtypeStruct((n_rows, D_MODEL), f32),
            jax.ShapeDtypeStruct((n_rows, D_MODEL), bf16),
            jax.ShapeDtypeStruct((n_rows, LANES), f32),
        ),
        grid=(n_tiles,),
        in_specs=[
            row(D_MODEL), row(W_Z), row(W_QB), row(W_GATE),
            full(W_Z, D_MODEL), full(W_QB, D_MODEL), full(D_MODEL, D_MODEL),
            modspec(2), full(1, D_MODEL), modspec(3), modspec(4),
            full(N_EXPERTS, D_MODEL), full(N_EXPERTS, D_MODEL), full(N_EXPERTS, 1),
        ],
        out_specs=(row(D_MODEL), row(D_MODEL), row(LANES)),
        compiler_params=pltpu.CompilerParams(
            dimension_semantics=("parallel",), vmem_limit_bytes=_vmem_limit(vmem)),
        name="merge",
    )(x_all, ya_all, yb_all, gate_all, w_oa, w_ob, w_o, mod_l, norm2_g.reshape(1, D_MODEL), mod_l, mod_l,
      wr_hi, wr_lo, router_bias.reshape(N_EXPERTS, 1))


def _moe_kernel(x1_ref, h2_ref, comb_ref, wg_ref, wu_ref, wd_ref, g2_ref, fg_ref, o_ref, acc_ref,
                *, sub, final_norm):
    e = pl.program_id(1)
    tm = h2_ref.shape[0]
    lane = lax.broadcasted_iota(jnp.int32, (sub, LANES), 1)

    @pl.when(e == 0)
    def _():
        acc_ref[...] = jnp.zeros_like(acc_ref)

    for r in range(tm // sub):
        rows = pl.ds(r * sub, sub)
        h = h2_ref[rows, :]
        a = _dot(h, wg_ref[...])
        b = _dot(h, wu_ref[...])
        he = (_silu(a) * b).astype(bf16)
        y = _dot(he, wd_ref[...])
        c = jnp.sum(jnp.where(lane == e, comb_ref[rows, :], 0.0), axis=-1, keepdims=True)
        acc_ref[rows, :] = acc_ref[rows, :] + c * y

    @pl.when(e == N_EXPERTS - 1)
    def _():
        x2 = x1_ref[...] + g2_ref[...] * acc_ref[...]
        if final_norm:
            ms = jnp.mean(x2 * x2, axis=-1, keepdims=True)
            x2 = x2 * lax.rsqrt(ms + EPS) * fg_ref[...]
        o_ref[...] = x2


def moe(x1, h2, comb, w_gate, w_up, w_down, mod_l, final_g, *, n_lat_tiles_per_seq, n_lat_tiles, batch, tm,
        final_norm):
    n_rows = x1.shape[0]
    n_tiles = n_rows // tm

    def seg(i):
        return jnp.where(i < n_lat_tiles, i // n_lat_tiles_per_seq, batch)

    row = lambda w: pl.BlockSpec((tm, w), lambda i, e: (i, 0))
    kern = functools.partial(_moe_kernel, sub=256, final_norm=final_norm)
    vmem = 2 * 3 * D_MODEL * D_EXPERT * 2 + 2 * tm * (D_MODEL * 4 * 2 + D_MODEL * 2 + LANES * 4) + tm * D_MODEL * 4 \
        + 8 * 1024 * 1024
    return pl.pallas_call(
        kern,
        out_shape=jax.ShapeDtypeStruct((n_rows, D_MODEL), f32),
        grid=(n_tiles, N_EXPERTS),
        in_specs=[
            row(D_MODEL), row(D_MODEL), row(LANES),
            pl.BlockSpec((None, D_MODEL, D_EXPERT), lambda i, e: (e, 0, 0)),
            pl.BlockSpec((None, D_MODEL, D_EXPERT), lambda i, e: (e, 0, 0)),
            pl.BlockSpec((None, D_EXPERT, D_MODEL), lambda i, e: (e, 0, 0)),
            pl.BlockSpec((None, 1, D_MODEL), lambda i, e: (seg(i), 0, 5)),
            pl.BlockSpec((1, D_MODEL), lambda i, e: (0, 0)),
        ],
        out_specs=row(D_MODEL),
        scratch_shapes=[pltpu.VMEM((tm, D_MODEL), f32)],
        compiler_params=pltpu.CompilerParams(
            dimension_semantics=("parallel", "arbitrary"), vmem_limit_bytes=_vmem_limit(vmem)),
        name="moe",
    )(x1, h2, comb, w_gate, w_up, w_down, mod_l, final_g.reshape(1, D_MODEL))


def _prep_w_in(w_in_l):
    qkv_a = 3 * H_A * DK_A
    o_z = qkv_a
    o_b = o_z + H_A * DV_A
    o_a = o_b + 2 * H_A
    o_q = o_a + 2 * H_A
    o_k = o_q + HQ_B * DH_B
    o_v = o_k + HKV_B * DH_B
    o_ga = o_v + HKV_B * DH_B
    w_ba = jnp.zeros((D_MODEL, W_BA), f32).at[:, :4 * H_A].set(w_in_l[:, o_b:o_q])
    wq = w_in_l[:, o_q:o_k].reshape(D_MODEL, HQ_B, DH_B)[:, np.array(HEAD_PERM), :].reshape(D_MODEL, W_QB)
    wq = wq * (DH_B ** -0.5)
    cat = jnp.concatenate(
        [w_in_l[:, :qkv_a], w_in_l[:, o_z:o_b], w_ba, wq, w_in_l[:, o_k:o_v], w_in_l[:, o_v:o_ga],
         w_in_l[:, o_ga:]], axis=1)
    return cat.astype(bf16)


def _rope_tables(seq, tm):
    t = np.arange(seq)
    rows = (t // GRID_W).astype(np.float32)
    cols = (t % GRID_W).astype(np.float32)
    nf = DH_B // 4
    inv = jnp.asarray(ROPE_THETA, f32) ** (-jnp.arange(nf, dtype=f32) / nf)
    ang_r = jnp.asarray(rows)[:, None] * inv[None, :]
    ang_c = jnp.asarray(cols)[:, None] * inv[None, :]
    cos64 = jnp.concatenate([jnp.cos(ang_r), jnp.cos(ang_r), jnp.cos(ang_c), jnp.cos(ang_c)], axis=1)
    zero = jnp.zeros_like(ang_r)
    sa64 = jnp.concatenate([-jnp.sin(ang_r), zero, -jnp.sin(ang_c), zero], axis=1)
    sb64 = jnp.concatenate([zero, jnp.sin(ang_r), zero, jnp.sin(ang_c)], axis=1)

    def fin(tab, fill):
        tab = jnp.concatenate([tab, tab], axis=1)
        return jnp.concatenate([tab, jnp.full((tm, LANES), fill, f32)], axis=0)

    return fin(cos64, 1.0), fin(sa64, 0.0), fin(sb64, 0.0)


TM_TOKEN = 256
TM_MOE = 1024


def kernel(x, c, ctx, c_ctx, w_mod, b_mod, norm1_g, norm2_g, w_in, conv_w, a_log, dt_bias, gdn_norm_g, sink,
           w_oa, w_ob, w_o, w_router, router_bias, w_gate, w_up, w_down, final_g):
    batch, seq, d = x.shape
    ctx_len = ctx.shape[1]
    depth = w_mod.shape[0]
    assert d == D_MODEL and seq % SUPER == 0 and ctx_len % SUPER == 0 and seq % TM_MOE == 0
    t_lat = batch * seq
    t_ctx = batch * ctx_len
    assert t_ctx % TM_MOE == 0

    x_all = jnp.concatenate([x.reshape(t_lat, d), ctx.reshape(t_ctx, d)], axis=0)
    t_all = t_lat + t_ctx

    n_cond = ((batch + 1 + SUBLANES - 1) // SUBLANES) * SUBLANES
    cond = jnp.zeros((n_cond, d), f32).at[:batch].set(c).at[batch].set(c_ctx)
    mod = adaln_all(cond, w_mod, b_mod).reshape(depth, n_cond, 1, 6 * d)

    rope_tabs = _rope_tables(seq, TM_TOKEN)
    wr_hi = w_router.T.astype(bf16)
    wr_lo = (w_router.T - wr_hi.astype(f32)).astype(bf16)
    perm_rows = np.concatenate([np.arange(h * DH_B, (h + 1) * DH_B) for h in HEAD_PERM])
    zero_state = jnp.zeros((batch, H_A, DK_A, DV_A), f32)

    tok = dict(n_lat_tiles_per_seq=seq // TM_TOKEN, n_lat_tiles=t_lat // TM_TOKEN, batch=batch, tm=TM_TOKEN)
    tok_moe = dict(n_lat_tiles_per_seq=seq // TM_MOE, n_lat_tiles=t_lat // TM_MOE, batch=batch, tm=TM_MOE)

    for l in range(depth):
        need_ctx = l < depth - 1
        w_cat = _prep_w_in(w_in[l])
        qkv_all, z_all, ba_all, qb_all, kv_all, gate_all = inproj(
            x_all, mod[l], norm1_g[l], w_cat, rope_tabs, **tok)

        ya_all = jnp.zeros((t_all, W_Z), bf16)
        ya_all, s_f, s_b = gdn(qkv_all, ba_all, z_all, ya_all, conv_w[l], a_log[l], dt_bias[l], gdn_norm_g[l],
                               zero_state, zero_state, batch=batch, seq=ctx_len, row_block0=t_lat // ctx_len)
        ya_all, _, _ = gdn(qkv_all, ba_all, z_all, ya_all, conv_w[l], a_log[l], dt_bias[l], gdn_norm_g[l],
                           s_f, s_b, batch=batch, seq=seq, row_block0=0)

        yb_all = attention(qb_all, kv_all, sink[l], batch=batch, seq=seq, ctx_len=ctx_len,
                           with_ctx_queries=need_ctx)

        n_rows = t_all if need_ctx else t_lat
        x1, h2, comb = merge(
            x_all, ya_all, yb_all, gate_all, w_oa[l].astype(bf16), w_ob[l][perm_rows].astype(bf16),
            w_o[l].astype(bf16), mod[l], norm2_g[l], wr_hi, wr_lo, router_bias, n_rows=n_rows, **tok)
        x_all = moe(x1, h2, comb, w_gate[l].astype(bf16), w_up[l].astype(bf16), w_down[l].astype(bf16),
                    mod[l], final_g, final_norm=not need_ctx, **tok_moe)

    return x_all[:t_lat].reshape(batch, seq, d)
```

```python
import functools

import jax
import jax.numpy as jnp
import numpy as np
from jax import lax
from jax.experimental import pallas as pl
from jax.experimental.pallas import tpu as pltpu

f32 = jnp.float32
bf16 = jnp.bfloat16

D_MODEL = 1024
GRID_W = 64
EPS = 1e-6
H_A = 4
DK_A = 128
DV_A = 128
CONV_K = 5
CHUNK = 64
HQ_B = 8
HKV_B = 2
DH_B = 64
WINDOW = 128
BLK = 128
ROPE_THETA = 10000.0
N_EXPERTS = 16
N_GROUPS = 4
EXP_PER_GROUP = 4
D_EXPERT = 512

LANES = 128
SUBLANES = 8
BF16_ROWS = 16
VMEM_BYTES_V7X = 64 * 1024 * 1024

W_QKV = 3 * H_A * DK_A
W_Z = H_A * DV_A
W_BG = LANES
W_QB = HQ_B * DH_B
W_KV = 2 * HKV_B * DH_B
W_GATE = 2 * D_MODEL
OFF_Z = W_QKV
OFF_BG = OFF_Z + W_Z
OFF_QB = OFF_BG + W_BG
OFF_KV = OFF_QB + W_QB
OFF_GATE = OFF_KV + W_KV
W_TOTAL = OFF_GATE + W_GATE
W_ROPE = W_QB + HKV_B * DH_B

NEG_BIG = -1e30
HEAD_PERM = (0, 4, 1, 5, 2, 6, 3, 7)

SUPER = 4 * CHUNK


def _vmem_limit(nbytes):
    return int(min(max(nbytes, 16 * 1024 * 1024), VMEM_BYTES_V7X - 8 * 1024 * 1024))


def _split3(x):
    hi = x.astype(bf16)
    r1 = x - hi.astype(f32)
    mid = r1.astype(bf16)
    lo = (r1 - mid.astype(f32)).astype(bf16)
    return hi, mid, lo


def _dot(a, b):
    return jnp.dot(a, b, preferred_element_type=f32)


def _dot_nt(a, b):
    return lax.dot_general(a, b, (((1,), (1,)), ((), ())), preferred_element_type=f32)


def _dot_exact_lhs_mask(mask_bf, parts):
    return _dot(mask_bf, parts[0]) + _dot(mask_bf, parts[1]) + _dot(mask_bf, parts[2])


def _sigmoid(x):
    return 1.0 / (1.0 + jnp.exp(-x))


def _silu(x):
    return x * _sigmoid(x)


def _softplus(x):
    return jnp.maximum(x, 0.0) + jnp.log1p(jnp.exp(-jnp.abs(x)))


def _adaln_kernel(c_ref, w_ref, b_ref, o_ref):
    c = c_ref[...]
    s = _silu(c).astype(bf16)
    o_ref[...] = _dot(s, w_ref[...].astype(bf16)) + b_ref[...]


def adaln_all(cond, w_mod, b_mod):
    depth = w_mod.shape[0]
    r = cond.shape[0]
    nblk = 6
    return pl.pallas_call(
        _adaln_kernel,
        out_shape=jax.ShapeDtypeStruct((depth, r, 6 * D_MODEL), f32),
        grid=(depth, nblk),
        in_specs=[
            pl.BlockSpec((r, D_MODEL), lambda l, j: (0, 0)),
            pl.BlockSpec((None, D_MODEL, D_MODEL), lambda l, j: (l, 0, j)),
            pl.BlockSpec((None, 1, D_MODEL), lambda l, j: (l, 0, j)),
        ],
        out_specs=pl.BlockSpec((None, r, D_MODEL), lambda l, j: (l, 0, j)),
        compiler_params=pltpu.CompilerParams(
            dimension_semantics=("parallel", "parallel"),
            vmem_limit_bytes=_vmem_limit(24 * 1024 * 1024)),
        name="adaln",
    )(cond, w_mod, b_mod.reshape(depth, 1, 6 * D_MODEL))


def _inproj_kernel(x_ref, g_ref, sh_ref, sc_ref, cos_ref, sa_ref, sb_ref, alog_ref, dtb_ref, w_ref,
                   qkv_ref, z_ref, bg_ref, qb_ref, kv_ref, gate_ref):
    x = x_ref[...]
    ms = jnp.mean(x * x, axis=-1, keepdims=True)
    y = x * lax.rsqrt(ms + EPS) * g_ref[...]
    h = (y * (1.0 + sc_ref[...]) + sh_ref[...]).astype(bf16)

    qkv_ref[...] = _dot(h, w_ref[:, 0:W_QKV]).astype(bf16)
    z_ref[...] = _dot(h, w_ref[:, OFF_Z:OFF_Z + W_Z]).astype(bf16)
    raw = _dot(h, w_ref[:, OFF_BG:OFF_BG + W_BG])
    lane = lax.broadcasted_iota(jnp.int32, raw.shape, 1)
    bg_ref[...] = jnp.where(lane < 2 * H_A, _sigmoid(raw),
                            -jnp.exp(alog_ref[...]) * _softplus(raw + dtb_ref[...]))

    pr = _dot(h, w_ref[:, OFF_QB:OFF_QB + W_ROPE])
    reps = W_ROPE // LANES
    cos = jnp.concatenate([cos_ref[...]] * reps, axis=1)
    sa = jnp.concatenate([sa_ref[...]] * reps, axis=1)
    sb = jnp.concatenate([sb_ref[...]] * reps, axis=1)
    half = DH_B // 4
    rot = pr * cos + pltpu.roll(pr, W_ROPE - half, 1) * sa + pltpu.roll(pr, half, 1) * sb
    qb_ref[...] = rot[:, 0:W_QB].astype(bf16)
    kv_ref[:, 0:LANES] = rot[:, W_QB:W_ROPE].astype(bf16)
    kv_ref[:, LANES:2 * LANES] = _dot(h, w_ref[:, OFF_KV + LANES:OFF_KV + 2 * LANES]).astype(bf16)

    gate_ref[...] = _sigmoid(_dot(h, w_ref[:, OFF_GATE:OFF_GATE + W_GATE])).astype(bf16)


def inproj(x_all, mod_l, norm_g, w_cat, rope_tabs, a_log_l, dt_bias_l, *, n_lat_tiles_per_seq, n_lat_tiles, batch,
           tm):
    t_all = x_all.shape[0]
    n_tiles = t_all // tm
    cos_t, sa_t, sb_t = rope_tabs
    pad_lanes = lambda v: jnp.zeros((1, LANES), f32).at[0, 2 * H_A:4 * H_A].set(v.reshape(-1))
    alog_v = pad_lanes(a_log_l)
    dtb_v = pad_lanes(dt_bias_l)

    def seg(i):
        return jnp.where(i < n_lat_tiles, i // n_lat_tiles_per_seq, batch)

    def pos(i):
        return jnp.where(i < n_lat_tiles, i % n_lat_tiles_per_seq, n_lat_tiles_per_seq)

    row = lambda w: pl.BlockSpec((tm, w), lambda i: (i, 0))
    tab = pl.BlockSpec((tm, LANES), lambda i: (pos(i), 0))
    out_shapes = (
        jax.ShapeDtypeStruct((t_all, W_QKV), bf16),
        jax.ShapeDtypeStruct((t_all, W_Z), bf16),
        jax.ShapeDtypeStruct((t_all, W_BG), f32),
        jax.ShapeDtypeStruct((t_all, W_QB), bf16),
        jax.ShapeDtypeStruct((t_all, W_KV), bf16),
        jax.ShapeDtypeStruct((t_all, W_GATE), bf16),
    )
    vmem = 2 * (W_TOTAL * D_MODEL * 2) + 2 * tm * (D_MODEL * 4 + W_TOTAL * 2 + 3 * LANES * 4) + tm * W_TOTAL * 4
    return pl.pallas_call(
        _inproj_kernel,
        out_shape=out_shapes,
        grid=(n_tiles,),
        in_specs=[
            row(D_MODEL),
            pl.BlockSpec((1, D_MODEL), lambda i: (0, 0)),
            pl.BlockSpec((None, 1, D_MODEL), lambda i: (seg(i), 0, 0)),
            pl.BlockSpec((None, 1, D_MODEL), lambda i: (seg(i), 0, 1)),
            tab, tab, tab,
            pl.BlockSpec((1, LANES), lambda i: (0, 0)),
            pl.BlockSpec((1, LANES), lambda i: (0, 0)),
            pl.BlockSpec((D_MODEL, W_TOTAL), lambda i: (0, 0)),
        ],
        out_specs=(row(W_QKV), row(W_Z), row(W_BG), row(W_QB), row(W_KV), row(W_GATE)),
        compiler_params=pltpu.CompilerParams(
            dimension_semantics=("parallel",), vmem_limit_bytes=_vmem_limit(vmem)),
        name="inproj",
    )(x_all, norm_g.reshape(1, D_MODEL), mod_l, mod_l, cos_t, sa_t, sb_t, alog_v, dtb_v, w_cat)


def _gdn_kernel(q_ref, k_ref, v_ref, bg_ref, z_ref, cwq_ref, cwk_ref, cwv_ref, gn_ref, s0f_ref, s0b_ref,
                yin_ref, y_ref, sf_ref, sb_ref,
                u_s, w_s, qg_s, ak_s, gl_s, o_s, st_s, *, seq, nsuper):
    del yin_ref
    hd = pl.program_id(1)
    nch = seq // CHUNK
    cps = SUPER // CHUNK

    ii = lax.broadcasted_iota(jnp.int32, (SUPER, SUPER), 0)
    jj = lax.broadcasted_iota(jnp.int32, (SUPER, SUPER), 1)
    same = (ii // CHUNK) == (jj // CHUNK)
    blk8 = (ii // SUBLANES) == (jj // SUBLANES)
    tri = (same & (jj <= ii), same & (jj >= ii))
    tri_s = (same & (jj < ii), same & (jj > ii))
    cum2_bf = jnp.concatenate([jnp.where(m, 1.0, 0.0) for m in tri], axis=0).astype(bf16)
    lane = lax.broadcasted_iota(jnp.int32, (SUPER, LANES), 1)

    def mm(a, b):
        return _dot(a.astype(bf16), b.astype(bf16))

    def conv_silu(ref, cw_ref, base, st):
        main = ref[pl.ds(base, SUPER), :].astype(f32)
        p0 = pl.multiple_of(jnp.maximum(base - BF16_ROWS, 0), BF16_ROWS)
        n0 = pl.multiple_of(jnp.minimum(base + SUPER, seq - BF16_ROWS), BF16_ROWS)
        prev = ref[pl.ds(p0, BF16_ROWS), :].astype(f32) * (st > 0).astype(f32)
        nxt = ref[pl.ds(n0, BF16_ROWS), :].astype(f32) * (st < nsuper - 1).astype(f32)
        win = jnp.concatenate([prev, main, nxt], axis=0)
        acc = jnp.zeros((SUPER, LANES), f32)
        for i in range(CONV_K):
            lo_row = BF16_ROWS - CONV_K // 2 + i
            acc = acc + win[lo_row:lo_row + SUPER, :] * cw_ref[i:i + 1, :]
        return _silu(acc)

    def column(x, c):
        col = jnp.sum(jnp.where(lane == c, x, 0.0), axis=-1, keepdims=True)
        return jnp.broadcast_to(col, (SUPER, LANES))

    def precompute(st, carry):
        base = pl.multiple_of(st * SUPER, SUPER)
        q = conv_silu(q_ref, cwq_ref, base, st)
        k = conv_silu(k_ref, cwk_ref, base, st)
        v = conv_silu(v_ref, cwv_ref, base, st)
        q = q * lax.rsqrt(jnp.sum(q * q, axis=-1, keepdims=True) + EPS) * (DK_A ** -0.5)
        k = k * lax.rsqrt(jnp.sum(k * k, axis=-1, keepdims=True) + EPS)
        q_bf = q.astype(bf16)
        k_bf = k.astype(bf16)
        qkk = _dot_nt(jnp.concatenate([q_bf, k_bf], axis=0), k_bf)
        qk_raw = qkk[0:SUPER]
        kk_raw = qkk[SUPER:2 * SUPER]
        bg = bg_ref[pl.ds(base, SUPER), :]
        rows = pl.ds(base, SUPER)

        g_both = [column(bg, 2 * H_A + d * H_A + hd) for d in range(2)]
        parts = [jnp.concatenate(p, axis=1) for p in zip(_split3(g_both[0]), _split3(g_both[1]))]
        csum_both = _dot_exact_lhs_mask(cum2_bf, parts)

        for d in range(2):
            beta = column(bg, d * H_A + hd)
            mask, mask_s = tri[d], tri_s[d]
            csum = csum_both[d * SUPER:(d + 1) * SUPER, d * LANES:(d + 1) * LANES]
            ct = jnp.concatenate([csum[0:LANES].T, csum[LANES:2 * LANES].T], axis=1)
            ct = jnp.concatenate([ct, ct], axis=0)
            c2 = jnp.concatenate([csum, csum], axis=1)
            decay = jnp.where(mask, jnp.exp(jnp.where(mask, c2 - ct, 0.0)), 0.0)
            beta2 = jnp.concatenate([beta, beta], axis=1)
            a_mat = jnp.where(mask_s, beta2 * kk_raw * decay, 0.0)
            attn = jnp.where(mask, qk_raw * decay, 0.0)

            dm = jnp.where(blk8, a_mat, 0.0)
            nm = a_mat - dm
            d2 = mm(dm, dm)
            d3 = mm(dm, d2)
            x1m = d2 - dm - d3
            d4 = mm(d2, d2)
            tdm = x1m + d4 + mm(x1m, d4)
            m1 = nm + mm(tdm, nm)
            m2 = mm(m1, m1)
            m3 = mm(m1, m2)
            y1m = m2 - m1 - m3
            m4 = mm(m2, m2)
            ym = y1m + m4 + mm(y1m, m4)
            t_m = tdm + ym + mm(ym, tdm)

            last = (CHUNK - 1, 0)[d]
            tot = jnp.concatenate(
                [jnp.broadcast_to(csum[c * CHUNK + last:c * CHUNK + last + 1, :], (CHUNK, LANES))
                 for c in range(cps)], axis=0)
            egc = jnp.exp(csum)
            rhs = jnp.concatenate([v * beta, k * beta * egc], axis=1)
            sol = rhs + mm(t_m, rhs)
            u_s[d, rows, :] = sol[:, 0:DV_A].astype(bf16)
            w_s[d, rows, :] = sol[:, DV_A:2 * DV_A].astype(bf16)
            qg_s[d, rows, :] = (q * egc).astype(bf16)
            kg_t = (k * jnp.exp(tot - csum)).T
            glast = jnp.exp(tot)
            for c in range(cps):
                cs = slice(c * CHUNK, (c + 1) * CHUNK)
                ak_s[d, st * cps + c] = jnp.concatenate([attn[cs, cs], kg_t[:, cs]], axis=0).astype(bf16)
                gl_s[d, pl.ds(st * cps + c, 1), :] = glast[c * CHUNK:c * CHUNK + 1, :]
        return carry

    lax.fori_loop(0, nsuper, precompute, 0)

    st_s[0] = s0f_ref[...]
    st_s[1] = s0b_ref[...]

    def scan_step(n, accumulate):
        for d in range(2):
            c = n if d == 0 else nch - 1 - n
            rows = pl.ds(pl.multiple_of(c * CHUNK, CHUNK), CHUNK)
            s_old = st_s[d]
            ws = _dot(jnp.concatenate([w_s[d, rows, :], qg_s[d, rows, :]], axis=0), s_old.astype(bf16))
            v_new = (u_s[d, rows, :].astype(f32) - ws[0:CHUNK]).astype(bf16)
            ov = _dot(ak_s[d, c], v_new)
            o = ws[CHUNK:2 * CHUNK] + ov[0:CHUNK]
            st_s[d] = s_old * gl_s[d, pl.ds(c, 1), :] + ov[CHUNK:CHUNK + DK_A]
            if accumulate:
                o_s[rows, :] = o_s[rows, :] + o
            else:
                o_s[rows, :] = o

    def scan_a(n, carry):
        scan_step(n, False)
        return carry

    def scan_b(n, carry):
        scan_step(n, True)
        return carry

    lax.fori_loop(0, nch // 2, scan_a, 0)
    lax.fori_loop(nch // 2, nch, scan_b, 0)

    def finish(st, carry):
        rows = pl.ds(pl.multiple_of(st * SUPER, SUPER), SUPER)
        o = o_s[rows, :]
        yn = o * lax.rsqrt(jnp.mean(o * o, axis=-1, keepdims=True) + EPS) * gn_ref[...]
        y_ref[rows, :] = (yn * _silu(z_ref[rows, :].astype(f32))).astype(bf16)
        return carry

    lax.fori_loop(0, nsuper, finish, 0)
    sf_ref[...] = st_s[0]
    sb_ref[...] = st_s[1]


def gdn(qkv_all, bg_all, z_all, ya_all, conv_w, gdn_g, s0f, s0b, *, batch, seq, row_block0):
    nsuper = seq // SUPER
    nch = seq // CHUNK
    nch_pad = max(nch, SUBLANES)
    cw = jnp.zeros((3 * H_A, SUBLANES, LANES), f32).at[:, :CONV_K, :].set(
        conv_w.reshape(CONV_K, 3 * H_A, LANES).transpose(1, 0, 2))

    col = lambda off: pl.BlockSpec((seq, LANES), lambda b, h: (row_block0 + b, off + h))
    cwspec = lambda off: pl.BlockSpec((None, SUBLANES, LANES), lambda b, h: (off + h, 0, 0))
    st_spec = pl.BlockSpec((None, None, DK_A, DV_A), lambda b, h: (b, h, 0, 0))
    kern = functools.partial(_gdn_kernel, seq=seq, nsuper=nsuper)
    vmem = (2 * seq * LANES * (3 * 2 + 2 + 4 + 2) + seq * LANES * (6 * 2 + 6 * 2 + 4)
            + 4 * 4 * DK_A * DV_A * 4 + 20 * 1024 * 1024)
    return pl.pallas_call(
        kern,
        out_shape=(
            jax.ShapeDtypeStruct(ya_all.shape, bf16),
            jax.ShapeDtypeStruct((batch, H_A, DK_A, DV_A), f32),
            jax.ShapeDtypeStruct((batch, H_A, DK_A, DV_A), f32),
        ),
        grid=(batch, H_A),
        in_specs=[
            col(0), col(H_A), col(2 * H_A),
            pl.BlockSpec((seq, LANES), lambda b, h: (row_block0 + b, 0)),
            col(0),
            cwspec(0), cwspec(H_A), cwspec(2 * H_A),
            pl.BlockSpec((1, DV_A), lambda b, h: (0, 0)),
            st_spec, st_spec,
            pl.BlockSpec(memory_space=pl.ANY),
        ],
        out_specs=(col(0), st_spec, st_spec),
        scratch_shapes=[
            pltpu.VMEM((2, seq, LANES), bf16),
            pltpu.VMEM((2, seq, LANES), bf16),
            pltpu.VMEM((2, seq, LANES), bf16),
            pltpu.VMEM((2, nch, CHUNK + DK_A, CHUNK), bf16),
            pltpu.VMEM((2, nch_pad, LANES), f32),
            pltpu.VMEM((seq, LANES), f32),
            pltpu.VMEM((2, DK_A, DV_A), f32),
        ],
        input_output_aliases={11: 0},
        compiler_params=pltpu.CompilerParams(
            dimension_semantics=("parallel", "parallel"), vmem_limit_bytes=_vmem_limit(vmem)),
        name="gdn",
    )(qkv_all, qkv_all, qkv_all, bg_all, z_all, cw, cw, cw, gdn_g.reshape(1, DV_A), s0f, s0b, ya_all)


def _attn_kernel(sink_ref, q_ref, kp_ref, kc_ref, kn_ref, kx_ref, o_ref, *, n_lat_blocks, blocks_per_seq):
    i = pl.program_id(0)
    is_lat = i < n_lat_blocks
    n = i % blocks_per_seq
    kv = jnp.concatenate([kp_ref[...], kc_ref[...], kn_ref[...], kx_ref[...]], axis=0)
    k2 = kv[:, 0:LANES]
    v2 = kv[:, LANES:2 * LANES]
    nctx = kx_ref.shape[0]

    row = lax.broadcasted_iota(jnp.int32, (BLK, BLK), 0)
    col = lax.broadcasted_iota(jnp.int32, (BLK, BLK), 1)
    ok_prev = (col >= row) & is_lat & (n > 0)
    ok_cent = jnp.broadcast_to(is_lat, (BLK, BLK))
    ok_next = (col <= row) & is_lat & (n < blocks_per_seq - 1)
    zeros = jnp.zeros((BLK, BLK), f32)
    bias = jnp.concatenate(
        [jnp.where(ok_prev, zeros, NEG_BIG), jnp.where(ok_cent, zeros, NEG_BIG),
         jnp.where(ok_next, zeros, NEG_BIG), jnp.zeros((BLK, nctx), f32)], axis=1)
    bias2 = jnp.concatenate([bias, bias], axis=0)

    lane = lax.broadcasted_iota(jnp.int32, (BLK, LANES), 1)
    lo = lane < DH_B
    lo_bf = jnp.where(lo, 1.0, 0.0).astype(bf16)
    hi_bf = jnp.where(lo, 0.0, 1.0).astype(bf16)
    rows2 = lax.broadcasted_iota(jnp.int32, (2 * BLK, 1), 0)
    for p in range(HQ_B // 2):
        q2 = q_ref[:, p * LANES:(p + 1) * LANES]
        qs = jnp.concatenate([q2 * lo_bf, q2 * hi_bf], axis=0)
        s = _dot_nt(qs, k2) + bias2
        sink = jnp.where(rows2 < BLK, sink_ref[p], sink_ref[HQ_B // 2 + p])
        m = jnp.maximum(jnp.max(s, axis=-1, keepdims=True), sink)
        e = jnp.exp(s - m)
        den = jnp.sum(e, axis=-1, keepdims=True) + jnp.exp(sink - m)
        o = _dot(e.astype(bf16), v2) * (1.0 / den)
        o_ref[:, p * LANES:(p + 1) * LANES] = jnp.where(lo, o[0:BLK], o[BLK:2 * BLK]).astype(bf16)


def attention(qb_all, kv_all, sink, *, batch, seq, ctx_len, with_ctx_queries):
    t_lat = batch * seq
    bps = seq // BLK
    n_lat = t_lat // BLK
    cps = ctx_len // BLK
    n_blocks = n_lat + (batch * cps if with_ctx_queries else 0)
    ctx_blk0 = t_lat // ctx_len

    def bidx(i):
        return jnp.where(i < n_lat, i // bps, (i - n_lat) // cps)

    def nbr(delta):
        def f(i):
            n = i % bps
            j = jnp.clip(n + delta, 0, bps - 1)
            return jnp.where(i < n_lat, (i // bps) * bps + j, 0)
        return f

    kern = functools.partial(_attn_kernel, n_lat_blocks=n_lat, blocks_per_seq=bps)
    return pl.pallas_call(
        kern,
        out_shape=jax.ShapeDtypeStruct(qb_all.shape, bf16),
        grid=(n_blocks,),
        in_specs=[
            pl.BlockSpec(memory_space=pltpu.SMEM),
            pl.BlockSpec((BLK, W_QB), lambda i: (i, 0)),
            pl.BlockSpec((BLK, W_KV), lambda i: (nbr(-1)(i), 0)),
            pl.BlockSpec((BLK, W_KV), lambda i: (nbr(0)(i), 0)),
            pl.BlockSpec((BLK, W_KV), lambda i: (nbr(1)(i), 0)),
            pl.BlockSpec((ctx_len, W_KV), lambda i: (ctx_blk0 + bidx(i), 0)),
        ],
        out_specs=pl.BlockSpec((BLK, W_QB), lambda i: (i, 0)),
        compiler_params=pltpu.CompilerParams(
            dimension_semantics=("parallel",), vmem_limit_bytes=_vmem_limit(32 * 1024 * 1024)),
        name="attn",
    )(sink, qb_all, kv_all, kv_all, kv_all, kv_all)


def _merge_kernel(x_ref, ya_ref, yb_ref, gate_ref, woa_ref, wob_ref, wo_ref, g1_ref, n2_ref, sh2_ref, sc2_ref,
                  wrh_ref, wrl_ref, rb_ref, x1_ref, h2_ref, comb_ref):
    pa = _dot(ya_ref[...], woa_ref[...])
    pb = _dot(yb_ref[...], wob_ref[...])
    m = gate_ref[:, 0:D_MODEL].astype(f32) * pa + gate_ref[:, D_MODEL:2 * D_MODEL].astype(f32) * pb
    y = _dot(m.astype(bf16), wo_ref[...])
    x1 = x_ref[...] + g1_ref[...] * y
    x1_ref[...] = x1
    ms = jnp.mean(x1 * x1, axis=-1, keepdims=True)
    h2 = (x1 * lax.rsqrt(ms + EPS) * n2_ref[...]) * (1.0 + sc2_ref[...]) + sh2_ref[...]
    h_hi = h2.astype(bf16)
    h2_ref[...] = h_hi
    h_lo = (h2 - h_hi.astype(f32)).astype(bf16)

    logit = _dot_nt(wrh_ref[...], h_hi) + _dot_nt(wrl_ref[...], h_hi) + _dot_nt(wrh_ref[...], h_lo)
    score = _sigmoid(logit)
    sel = score + rb_ref[...]
    tm = sel.shape[1]

    def top2_in_group(g):
        r = [sel[g * EXP_PER_GROUP + j:g * EXP_PER_GROUP + j + 1, :] for j in range(EXP_PER_GROUP)]
        pair = None
        for a in range(EXP_PER_GROUP):
            for b in range(a + 1, EXP_PER_GROUP):
                s_ab = r[a] + r[b]
                pair = s_ab if pair is None else jnp.maximum(pair, s_ab)
        best = r[0]
        i1 = jnp.zeros((1, tm), jnp.int32)
        for j in range(1, EXP_PER_GROUP):
            better = r[j] > best
            best = jnp.where(better, r[j], best)
            i1 = jnp.where(better, j, i1)
        second = jnp.full((1, tm), -jnp.inf, f32)
        i2 = jnp.zeros((1, tm), jnp.int32)
        for j in range(EXP_PER_GROUP):
            cand = jnp.where(i1 == j, -jnp.inf, r[j])
            better = cand > second
            second = jnp.where(better, cand, second)
            i2 = jnp.where(better, j, i2)
        return pair, i1 + g * EXP_PER_GROUP, i2 + g * EXP_PER_GROUP

    gs, e1, e2 = top2_in_group(0)
    for g in range(1, N_GROUPS):
        gs_g, e1_g, e2_g = top2_in_group(g)
        better = gs_g > gs
        gs = jnp.where(better, gs_g, gs)
        e1 = jnp.where(better, e1_g, e1)
        e2 = jnp.where(better, e2_g, e2)

    eidx = lax.broadcasted_iota(jnp.int32, (N_EXPERTS, tm), 0)
    hit1 = eidx == e1
    hit2 = eidx == e2
    s1 = jnp.sum(jnp.where(hit1, score, 0.0), axis=0, keepdims=True)
    s2 = jnp.sum(jnp.where(hit2, score, 0.0), axis=0, keepdims=True)
    tot = s1 + s2
    comb_t = jnp.where(hit1, s1 / tot, 0.0) + jnp.where(hit2, s2 / tot, 0.0)
    comb_pad = jnp.concatenate([comb_t, jnp.zeros((LANES - N_EXPERTS, tm), f32)], axis=0)
    comb_ref[...] = comb_pad.T


def merge(x_all, ya_all, yb_all, gate_all, w_oa, w_ob, w_o, mod_l, norm2_g, wr_hi, wr_lo, router_bias,
          *, n_rows, n_lat_tiles_per_seq, n_lat_tiles, batch, tm):
    n_tiles = n_rows // tm

    def seg(i):
        return jnp.where(i < n_lat_tiles, i // n_lat_tiles_per_seq, batch)

    row = lambda w: pl.BlockSpec((tm, w), lambda i: (i, 0))
    full = lambda a, b: pl.BlockSpec((a, b), lambda i: (0, 0))
    modspec = lambda j: pl.BlockSpec((None, 1, D_MODEL), lambda i: (seg(i), 0, j))
    vmem = 2 * (2 * W_Z * D_MODEL * 2 + D_MODEL * D_MODEL * 2) + 2 * tm * (
        D_MODEL * 4 * 2 + 2 * W_Z * 2 + W_GATE * 2 + D_MODEL * 2 + LANES * 4) + 8 * tm * D_MODEL * 4
    return pl.pallas_call(
        _merge_kernel,
        out_shape=(
            jax.ShapeDtypeStruct((n_rows, D_MODEL), f32),
            jax.ShapeDtypeStruct((n_rows, D_MODEL), bf16),
            jax.ShapeDtypeStruct((n_rows, LANES), f32),
        ),
        grid=(n_tiles,),
        in_specs=[
            row(D_MODEL), row(W_Z), row(W_QB), row(W_GATE),
            full(W_Z, D_MODEL), full(W_QB, D_MODEL), full(D_MODEL, D_MODEL),
            modspec(2), full(1, D_MODEL), modspec(3), modspec(4),
            full(N_EXPERTS, D_MODEL), full(N_EXPERTS, D_MODEL), full(N_EXPERTS, 1),
        ],
        out_specs=(row(D_MODEL), row(D_MODEL), row(LANES)),
        compiler_params=pltpu.CompilerParams(
            dimension_semantics=("parallel",), vmem_limit_bytes=_vmem_limit(vmem)),
        name="merge",
    )(x_all, ya_all, yb_all, gate_all, w_oa, w_ob, w_o, mod_l, norm2_g.reshape(1, D_MODEL), mod_l, mod_l,
      wr_hi, wr_lo, router_bias.reshape(N_EXPERTS, 1))


def _moe_kernel(x1_ref, h2_ref, comb_ref, wg_ref, wu_ref, wd_ref, g2_ref, fg_ref, o_ref, acc_ref,
                *, sub, final_norm):
    e = pl.program_id(1)
    tm = h2_ref.shape[0]
    lane = lax.broadcasted_iota(jnp.int32, (sub, LANES), 1)

    @pl.when(e == 0)
    def _():
        acc_ref[...] = jnp.zeros_like(acc_ref)

    for r in range(tm // sub):
        rows = pl.ds(r * sub, sub)
        h = h2_ref[rows, :]
        a = _dot(h, wg_ref[...])
        b = _dot(h, wu_ref[...])
        he = (_silu(a) * b).astype(bf16)
        y = _dot(he, wd_ref[...])
        c = jnp.sum(jnp.where(lane == e, comb_ref[rows, :], 0.0), axis=-1, keepdims=True)
        acc_ref[rows, :] = acc_ref[rows, :] + c * y

    @pl.when(e == N_EXPERTS - 1)
    def _():
        x2 = x1_ref[...] + g2_ref[...] * acc_ref[...]
        if final_norm:
            ms = jnp.mean(x2 * x2, axis=-1, keepdims=True)
            x2 = x2 * lax.rsqrt(ms + EPS) * fg_ref[...]
        o_ref[...] = x2


def moe(x1, h2, comb, w_gate, w_up, w_down, mod_l, final_g, *, n_lat_tiles_per_seq, n_lat_tiles, batch, tm,
        final_norm):
    n_rows = x1.shape[0]
    n_tiles = n_rows // tm

    def seg(i):
        return jnp.where(i < n_lat_tiles, i // n_lat_tiles_per_seq, batch)

    row = lambda w: pl.BlockSpec((tm, w), lambda i, e: (i, 0))
    kern = functools.partial(_moe_kernel, sub=256, final_norm=final_norm)
    vmem = 2 * 3 * D_MODEL * D_EXPERT * 2 + 2 * tm * (D_MODEL * 4 * 2 + D_MODEL * 2 + LANES * 4) + tm * D_MODEL * 4 \
        + 8 * 1024 * 1024
    return pl.pallas_call(
        kern,
        out_shape=jax.ShapeDtypeStruct((n_rows, D_MODEL), f32),
        grid=(n_tiles, N_EXPERTS),
        in_specs=[
            row(D_MODEL), row(D_MODEL), row(LANES),
            pl.BlockSpec((None, D_MODEL, D_EXPERT), lambda i, e: (e, 0, 0)),
            pl.BlockSpec((None, D_MODEL, D_EXPERT), lambda i, e: (e, 0, 0)),
            pl.BlockSpec((None, D_EXPERT, D_MODEL), lambda i, e: (e, 0, 0)),
            pl.BlockSpec((None, 1, D_MODEL), lambda i, e: (seg(i), 0, 5)),
            pl.BlockSpec((1, D_MODEL), lambda i, e: (0, 0)),
        ],
        out_specs=row(D_MODEL),
        scratch_shapes=[pltpu.VMEM((tm, D_MODEL), f32)],
        compiler_params=pltpu.CompilerParams(
            dimension_semantics=("parallel", "arbitrary"), vmem_limit_bytes=_vmem_limit(vmem)),
        name="moe",
    )(x1, h2, comb, w_gate, w_up, w_down, mod_l, final_g.reshape(1, D_MODEL))


def _prep_w_in(w_in_l):
    qkv_a = 3 * H_A * DK_A
    o_z = qkv_a
    o_b = o_z + H_A * DV_A
    o_a = o_b + 2 * H_A
    o_q = o_a + 2 * H_A
    o_k = o_q + HQ_B * DH_B
    o_v = o_k + HKV_B * DH_B
    o_ga = o_v + HKV_B * DH_B
    w_bg = jnp.zeros((D_MODEL, W_BG), f32).at[:, :4 * H_A].set(w_in_l[:, o_b:o_q])
    wq = w_in_l[:, o_q:o_k].reshape(D_MODEL, HQ_B, DH_B)[:, np.array(HEAD_PERM), :].reshape(D_MODEL, W_QB)
    wq = wq * (DH_B ** -0.5)
    cat = jnp.concatenate(
        [w_in_l[:, :qkv_a], w_in_l[:, o_z:o_b], w_bg, wq, w_in_l[:, o_k:o_v], w_in_l[:, o_v:o_ga],
         w_in_l[:, o_ga:]], axis=1)
    return cat.astype(bf16)


def _rope_tables(seq, tm):
    t = np.arange(seq)
    rows = (t // GRID_W).astype(np.float32)
    cols = (t % GRID_W).astype(np.float32)
    nf = DH_B // 4
    inv = jnp.asarray(ROPE_THETA, f32) ** (-jnp.arange(nf, dtype=f32) / nf)
    ang_r = jnp.asarray(rows)[:, None] * inv[None, :]
    ang_c = jnp.asarray(cols)[:, None] * inv[None, :]
    cos64 = jnp.concatenate([jnp.cos(ang_r), jnp.cos(ang_r), jnp.cos(ang_c), jnp.cos(ang_c)], axis=1)
    zero = jnp.zeros_like(ang_r)
    sa64 = jnp.concatenate([-jnp.sin(ang_r), zero, -jnp.sin(ang_c), zero], axis=1)
    sb64 = jnp.concatenate([zero, jnp.sin(ang_r), zero, jnp.sin(ang_c)], axis=1)

    def fin(tab, fill):
        tab = jnp.concatenate([tab, tab], axis=1)
        return jnp.concatenate([tab, jnp.full((tm, LANES), fill, f32)], axis=0)

    return fin(cos64, 1.0), fin(sa64, 0.0), fin(sb64, 0.0)


TM_TOKEN = 256
TM_MOE = 1024


def kernel(x, c, ctx, c_ctx, w_mod, b_mod, norm1_g, norm2_g, w_in, conv_w, a_log, dt_bias, gdn_norm_g, sink,
           w_oa, w_ob, w_o, w_router, router_bias, w_gate, w_up, w_down, final_g):
    batch, seq, d = x.shape
    ctx_len = ctx.shape[1]
    depth = w_mod.shape[0]
    assert d == D_MODEL and seq % SUPER == 0 and ctx_len % SUPER == 0 and seq % TM_MOE == 0
    t_lat = batch * seq
    t_ctx = batch * ctx_len
    assert t_ctx % TM_MOE == 0

    x_all = jnp.concatenate([x.reshape(t_lat, d), ctx.reshape(t_ctx, d)], axis=0)
    t_all = t_lat + t_ctx

    n_cond = ((batch + 1 + SUBLANES - 1) // SUBLANES) * SUBLANES
    cond = jnp.zeros((n_cond, d), f32).at[:batch].set(c).at[batch].set(c_ctx)
    mod = adaln_all(cond, w_mod, b_mod).reshape(depth, n_cond, 1, 6 * d)

    rope_tabs = _rope_tables(seq, TM_TOKEN)
    wr_hi = w_router.T.astype(bf16)
    wr_lo = (w_router.T - wr_hi.astype(f32)).astype(bf16)
    perm_rows = np.concatenate([np.arange(h * DH_B, (h + 1) * DH_B) for h in HEAD_PERM])
    zero_state = jnp.zeros((batch, H_A, DK_A, DV_A), f32)

    tok = dict(n_lat_tiles_per_seq=seq // TM_TOKEN, n_lat_tiles=t_lat // TM_TOKEN, batch=batch, tm=TM_TOKEN)
    tok_moe = dict(n_lat_tiles_per_seq=seq // TM_MOE, n_lat_tiles=t_lat // TM_MOE, batch=batch, tm=TM_MOE)

    for l in range(depth):
        need_ctx = l < depth - 1
        w_cat = _prep_w_in(w_in[l])
        qkv_all, z_all, bg_all, qb_all, kv_all, gate_all = inproj(
            x_all, mod[l], norm1_g[l], w_cat, rope_tabs, a_log[l], dt_bias[l], **tok)

        ya_all = jnp.zeros((t_all, W_Z), bf16)
        ya_all, s_f, s_b = gdn(qkv_all, bg_all, z_all, ya_all, conv_w[l], gdn_norm_g[l], zero_state, zero_state,
                               batch=batch, seq=ctx_len, row_block0=t_lat // ctx_len)
        ya_all, _, _ = gdn(qkv_all, bg_all, z_all, ya_all, conv_w[l], gdn_norm_g[l], s_f, s_b,
                           batch=batch, seq=seq, row_block0=0)

        yb_all = attention(qb_all, kv_all, sink[l], batch=batch, seq=seq, ctx_len=ctx_len,
                           with_ctx_queries=need_ctx)

        n_rows = t_all if need_ctx else t_lat
        x1, h2, comb = merge(
            x_all, ya_all, yb_all, gate_all, w_oa[l].astype(bf16), w_ob[l][perm_rows].astype(bf16),
            w_o[l].astype(bf16), mod[l], norm2_g[l], wr_hi, wr_lo, router_bias, n_rows=n_rows, **tok)
        x_all = moe(x1, h2, comb, w_gate[l].astype(bf16), w_up[l].astype(bf16), w_down[l].astype(bf16),
                    mod[l], final_g, final_norm=not need_ctx, **tok_moe)

    return x_all[:t_lat].reshape(batch, seq, d)
```

```python
import functools

import jax
import jax.numpy as jnp
import numpy as np
from jax import lax
from jax.experimental import pallas as pl
from jax.experimental.pallas import tpu as pltpu

f32 = jnp.float32
bf16 = jnp.bfloat16

D_MODEL = 1024
GRID_W = 64
EPS = 1e-6
H_A = 4
DK_A = 128
DV_A = 128
CONV_K = 5
CHUNK = 64
HQ_B = 8
HKV_B = 2
DH_B = 64
WINDOW = 128
BLK = 128
ROPE_THETA = 10000.0
N_EXPERTS = 16
N_GROUPS = 4
EXP_PER_GROUP = 4
D_EXPERT = 512

LANES = 128
SUBLANES = 8
BF16_ROWS = 16
VMEM_BYTES_V7X = 64 * 1024 * 1024

W_QKV = 3 * H_A * DK_A
W_Z = H_A * DV_A
W_BG = LANES
W_QB = HQ_B * DH_B
W_KV = 2 * HKV_B * DH_B
W_GATE = 2 * D_MODEL
OFF_Z = W_QKV
OFF_BG = OFF_Z + W_Z
OFF_QB = OFF_BG + W_BG
OFF_KV = OFF_QB + W_QB
OFF_GATE = OFF_KV + W_KV
W_TOTAL = OFF_GATE + W_GATE
W_ROPE = W_QB + HKV_B * DH_B

NEG_BIG = -1e30
HEAD_PERM = (0, 4, 1, 5, 2, 6, 3, 7)

SUPER = 4 * CHUNK
MOE_CHUNK = 192


def _vmem_limit(nbytes):
    return int(min(max(nbytes, 16 * 1024 * 1024), VMEM_BYTES_V7X - 8 * 1024 * 1024))


def _split3(x):
    hi = x.astype(bf16)
    r1 = x - hi.astype(f32)
    mid = r1.astype(bf16)
    lo = (r1 - mid.astype(f32)).astype(bf16)
    return hi, mid, lo


def _dot(a, b):
    return jnp.dot(a, b, preferred_element_type=f32)


def _dot_nt(a, b):
    return lax.dot_general(a, b, (((1,), (1,)), ((), ())), preferred_element_type=f32)


def _dot_exact_lhs_mask(mask_bf, parts):
    return _dot(mask_bf, parts[0]) + _dot(mask_bf, parts[1]) + _dot(mask_bf, parts[2])


def _pack_halves(x):
    w = x.shape[1] // 2
    lo = lax.bitcast_convert_type(x[:, 0:w].astype(bf16).astype(f32), jnp.uint32)
    hi = lax.bitcast_convert_type(x[:, w:2 * w].astype(bf16).astype(f32), jnp.uint32)
    return lax.shift_right_logical(lo, jnp.uint32(16)) | (hi & jnp.uint32(0xFFFF0000))


def _unpack_halves(p):
    lo = lax.bitcast_convert_type(lax.shift_left(p, jnp.uint32(16)), f32)
    hi = lax.bitcast_convert_type(p & jnp.uint32(0xFFFF0000), f32)
    return lo, hi


def _sigmoid(x):
    return 1.0 / (1.0 + jnp.exp(-x))


def _silu(x):
    return x * _sigmoid(x)


def _softplus(x):
    return jnp.maximum(x, 0.0) + jnp.log1p(jnp.exp(-jnp.abs(x)))


def _adaln_kernel(c_ref, w_ref, b_ref, o_ref):
    c = c_ref[...]
    s = _silu(c).astype(bf16)
    o_ref[...] = _dot(s, w_ref[...].astype(bf16)) + b_ref[...]


def adaln_all(cond, w_mod, b_mod):
    depth = w_mod.shape[0]
    r = cond.shape[0]
    nblk = 6
    return pl.pallas_call(
        _adaln_kernel,
        out_shape=jax.ShapeDtypeStruct((depth, r, 6 * D_MODEL), f32),
        grid=(depth, nblk),
        in_specs=[
            pl.BlockSpec((r, D_MODEL), lambda l, j: (0, 0)),
            pl.BlockSpec((None, D_MODEL, D_MODEL), lambda l, j: (l, 0, j)),
            pl.BlockSpec((None, 1, D_MODEL), lambda l, j: (l, 0, j)),
        ],
        out_specs=pl.BlockSpec((None, r, D_MODEL), lambda l, j: (l, 0, j)),
        compiler_params=pltpu.CompilerParams(
            dimension_semantics=("parallel", "parallel"),
            vmem_limit_bytes=_vmem_limit(24 * 1024 * 1024)),
        name="adaln",
    )(cond, w_mod, b_mod.reshape(depth, 1, 6 * D_MODEL))


def _inproj_kernel(x_ref, g_ref, sh_ref, sc_ref, cos_ref, sa_ref, sb_ref, alog_ref, dtb_ref, w_ref,
                   qkv_ref, z_ref, bg_ref, qb_ref, kv_ref, gate_ref):
    x = x_ref[...]
    ms = jnp.mean(x * x, axis=-1, keepdims=True)
    y = x * lax.rsqrt(ms + EPS) * g_ref[...]
    h = (y * (1.0 + sc_ref[...]) + sh_ref[...]).astype(bf16)

    qkv_ref[...] = _dot(h, w_ref[:, 0:W_QKV]).astype(bf16)
    z_ref[...] = _dot(h, w_ref[:, OFF_Z:OFF_Z + W_Z]).astype(bf16)
    raw = _dot(h, w_ref[:, OFF_BG:OFF_BG + W_BG])
    lane = lax.broadcasted_iota(jnp.int32, raw.shape, 1)
    bg_ref[...] = jnp.where(lane < 2 * H_A, _sigmoid(raw),
                            -jnp.exp(alog_ref[...]) * _softplus(raw + dtb_ref[...]))

    pr = _dot(h, w_ref[:, OFF_QB:OFF_QB + W_ROPE])
    reps = W_ROPE // LANES
    cos = jnp.concatenate([cos_ref[...]] * reps, axis=1)
    sa = jnp.concatenate([sa_ref[...]] * reps, axis=1)
    sb = jnp.concatenate([sb_ref[...]] * reps, axis=1)
    half = DH_B // 4
    rot = pr * cos + pltpu.roll(pr, W_ROPE - half, 1) * sa + pltpu.roll(pr, half, 1) * sb
    qb_ref[...] = rot[:, 0:W_QB].astype(bf16)
    kv_ref[:, 0:LANES] = rot[:, W_QB:W_ROPE].astype(bf16)
    kv_ref[:, LANES:2 * LANES] = _dot(h, w_ref[:, OFF_KV + LANES:OFF_KV + 2 * LANES]).astype(bf16)

    gate_ref[...] = _sigmoid(_dot(h, w_ref[:, OFF_GATE:OFF_GATE + W_GATE])).astype(bf16)


def inproj(x_all, mod_l, norm_g, w_cat, rope_tabs, a_log_l, dt_bias_l, *, n_lat_tiles_per_seq, n_lat_tiles, batch,
           tm):
    t_all = x_all.shape[0]
    n_tiles = t_all // tm
    cos_t, sa_t, sb_t = rope_tabs
    pad_lanes = lambda v: jnp.zeros((1, LANES), f32).at[0, 2 * H_A:4 * H_A].set(v.reshape(-1))
    alog_v = pad_lanes(a_log_l)
    dtb_v = pad_lanes(dt_bias_l)

    def seg(i):
        return jnp.where(i < n_lat_tiles, i // n_lat_tiles_per_seq, batch)

    def pos(i):
        return jnp.where(i < n_lat_tiles, i % n_lat_tiles_per_seq, n_lat_tiles_per_seq)

    row = lambda w: pl.BlockSpec((tm, w), lambda i: (i, 0))
    tab = pl.BlockSpec((tm, LANES), lambda i: (pos(i), 0))
    out_shapes = (
        jax.ShapeDtypeStruct((t_all, W_QKV), bf16),
        jax.ShapeDtypeStruct((t_all, W_Z), bf16),
        jax.ShapeDtypeStruct((t_all, W_BG), f32),
        jax.ShapeDtypeStruct((t_all, W_QB), bf16),
        jax.ShapeDtypeStruct((t_all, W_KV), bf16),
        jax.ShapeDtypeStruct((t_all, W_GATE), bf16),
    )
    vmem = 2 * (W_TOTAL * D_MODEL * 2) + 2 * tm * (D_MODEL * 4 + W_TOTAL * 2 + 3 * LANES * 4) + tm * W_TOTAL * 4
    return pl.pallas_call(
        _inproj_kernel,
        out_shape=out_shapes,
        grid=(n_tiles,),
        in_specs=[
            row(D_MODEL),
            pl.BlockSpec((1, D_MODEL), lambda i: (0, 0)),
            pl.BlockSpec((None, 1, D_MODEL), lambda i: (seg(i), 0, 0)),
            pl.BlockSpec((None, 1, D_MODEL), lambda i: (seg(i), 0, 1)),
            tab, tab, tab,
            pl.BlockSpec((1, LANES), lambda i: (0, 0)),
            pl.BlockSpec((1, LANES), lambda i: (0, 0)),
            pl.BlockSpec((D_MODEL, W_TOTAL), lambda i: (0, 0)),
        ],
        out_specs=(row(W_QKV), row(W_Z), row(W_BG), row(W_QB), row(W_KV), row(W_GATE)),
        compiler_params=pltpu.CompilerParams(
            dimension_semantics=("parallel",), vmem_limit_bytes=_vmem_limit(vmem)),
        name="inproj",
    )(x_all, norm_g.reshape(1, D_MODEL), mod_l, mod_l, cos_t, sa_t, sb_t, alog_v, dtb_v, w_cat)


def _gdn_kernel(q_ref, k_ref, v_ref, bg_ref, z_ref, cwq_ref, cwk_ref, cwv_ref, gn_ref, s0f_ref, s0b_ref,
                yin_ref, y_ref, sf_ref, sb_ref,
                u_s, w_s, qg_s, ak_s, gl_s, o_s, st_s, *, seq, nsuper):
    del yin_ref
    hd = pl.program_id(1)
    nch = seq // CHUNK
    cps = SUPER // CHUNK

    ii = lax.broadcasted_iota(jnp.int32, (SUPER, SUPER), 0)
    jj = lax.broadcasted_iota(jnp.int32, (SUPER, SUPER), 1)
    same = (ii // CHUNK) == (jj // CHUNK)
    blk8 = (ii // SUBLANES) == (jj // SUBLANES)
    tri = (same & (jj <= ii), same & (jj >= ii))
    tri_s = (same & (jj < ii), same & (jj > ii))
    cum2_bf = jnp.concatenate([jnp.where(m, 1.0, 0.0) for m in tri], axis=0).astype(bf16)
    lane = lax.broadcasted_iota(jnp.int32, (SUPER, LANES), 1)

    def mm(a, b):
        return _dot(a.astype(bf16), b.astype(bf16))

    def conv_silu(ref, cw_ref, base, st):
        main = ref[pl.ds(base, SUPER), :].astype(f32)
        p0 = pl.multiple_of(jnp.maximum(base - BF16_ROWS, 0), BF16_ROWS)
        n0 = pl.multiple_of(jnp.minimum(base + SUPER, seq - BF16_ROWS), BF16_ROWS)
        prev = ref[pl.ds(p0, BF16_ROWS), :].astype(f32) * jnp.where(st > 0, 1.0, 0.0)
        nxt = ref[pl.ds(n0, BF16_ROWS), :].astype(f32) * jnp.where(st < nsuper - 1, 1.0, 0.0)
        win = jnp.concatenate([prev, main, nxt], axis=0)
        acc = jnp.zeros((SUPER, LANES), f32)
        for i in range(CONV_K):
            lo_row = BF16_ROWS - CONV_K // 2 + i
            acc = acc + win[lo_row:lo_row + SUPER, :] * cw_ref[i:i + 1, :]
        return _silu(acc)

    def column(x, c):
        col = jnp.sum(jnp.where(lane == c, x, 0.0), axis=-1, keepdims=True)
        return jnp.broadcast_to(col, (SUPER, LANES))

    def precompute(st, carry):
        base = pl.multiple_of(st * SUPER, SUPER)
        q = conv_silu(q_ref, cwq_ref, base, st)
        k = conv_silu(k_ref, cwk_ref, base, st)
        v = conv_silu(v_ref, cwv_ref, base, st)
        q = q * lax.rsqrt(jnp.sum(q * q, axis=-1, keepdims=True) + EPS) * (DK_A ** -0.5)
        k = k * lax.rsqrt(jnp.sum(k * k, axis=-1, keepdims=True) + EPS)
        q_bf = q.astype(bf16)
        k_bf = k.astype(bf16)
        qkk = _dot_nt(jnp.concatenate([q_bf, k_bf], axis=0), k_bf)
        qk_raw = qkk[0:SUPER]
        kk_raw = qkk[SUPER:2 * SUPER]
        bg = bg_ref[pl.ds(base, SUPER), :]
        rows = pl.ds(base, SUPER)

        g_both = [column(bg, 2 * H_A + d * H_A + hd) for d in range(2)]
        parts = [jnp.concatenate(p, axis=1) for p in zip(_split3(g_both[0]), _split3(g_both[1]))]
        csum_both = _dot_exact_lhs_mask(cum2_bf, parts)

        for d in range(2):
            beta = column(bg, d * H_A + hd)
            mask, mask_s = tri[d], tri_s[d]
            csum = csum_both[d * SUPER:(d + 1) * SUPER, d * LANES:(d + 1) * LANES]
            ct = jnp.concatenate([csum[0:LANES].T, csum[LANES:2 * LANES].T], axis=1)
            ct = jnp.concatenate([ct, ct], axis=0)
            c2 = jnp.concatenate([csum, csum], axis=1)
            decay = jnp.where(mask, jnp.exp(jnp.where(mask, c2 - ct, 0.0)), 0.0)
            beta2 = jnp.concatenate([beta, beta], axis=1)
            a_mat = jnp.where(mask_s, beta2 * kk_raw * decay, 0.0)
            attn = jnp.where(mask, qk_raw * decay, 0.0)

            dm = jnp.where(blk8, a_mat, 0.0)
            nm = a_mat - dm
            d2 = mm(dm, dm)
            d3 = mm(dm, d2)
            x1m = d2 - dm - d3
            d4 = mm(d2, d2)
            tdm = x1m + d4 + mm(x1m, d4)
            m1 = nm + mm(tdm, nm)
            m2 = mm(m1, m1)
            m3 = mm(m1, m2)
            y1m = m2 - m1 - m3
            m4 = mm(m2, m2)
            ym = y1m + m4 + mm(y1m, m4)
            t_m = tdm + ym + mm(ym, tdm)

            last = (CHUNK - 1, 0)[d]
            tot = jnp.concatenate(
                [jnp.broadcast_to(csum[c * CHUNK + last:c * CHUNK + last + 1, :], (CHUNK, LANES))
                 for c in range(cps)], axis=0)
            egc = jnp.exp(csum)
            rhs = jnp.concatenate([v * beta, k * beta * egc], axis=1)
            sol = rhs + mm(t_m, rhs)
            u_s[d, rows, :] = sol[:, 0:DV_A].astype(bf16)
            w_s[d, rows, :] = sol[:, DV_A:2 * DV_A].astype(bf16)
            qg_s[d, rows, :] = (q * egc).astype(bf16)
            kg_t = (k * jnp.exp(tot - csum)).T
            glast = jnp.exp(tot)
            for c in range(cps):
                cs = slice(c * CHUNK, (c + 1) * CHUNK)
                ak_s[d, st * cps + c] = jnp.concatenate([attn[cs, cs], kg_t[:, cs]], axis=0).astype(bf16)
                gl_s[d, pl.ds(st * cps + c, 1), :] = glast[c * CHUNK:c * CHUNK + 1, :]
        return carry

    lax.fori_loop(0, nsuper, precompute, 0)

    st_s[0] = s0f_ref[...]
    st_s[1] = s0b_ref[...]

    def scan_step(n, accumulate):
        for d in range(2):
            c = n if d == 0 else nch - 1 - n
            rows = pl.ds(pl.multiple_of(c * CHUNK, CHUNK), CHUNK)
            s_old = st_s[d]
            ws = _dot(jnp.concatenate([w_s[d, rows, :], qg_s[d, rows, :]], axis=0), s_old.astype(bf16))
            v_new = (u_s[d, rows, :].astype(f32) - ws[0:CHUNK]).astype(bf16)
            ov = _dot(ak_s[d, c], v_new)
            o = ws[CHUNK:2 * CHUNK] + ov[0:CHUNK]
            st_s[d] = s_old * gl_s[d, pl.ds(c, 1), :] + ov[CHUNK:CHUNK + DK_A]
            if accumulate:
                o_s[rows, :] = o_s[rows, :] + o
            else:
                o_s[rows, :] = o

    def scan_a(n, carry):
        scan_step(n, False)
        return carry

    def scan_b(n, carry):
        scan_step(n, True)
        return carry

    lax.fori_loop(0, nch // 2, scan_a, 0)
    lax.fori_loop(nch // 2, nch, scan_b, 0)

    def finish(st, carry):
        rows = pl.ds(pl.multiple_of(st * SUPER, SUPER), SUPER)
        o = o_s[rows, :]
        yn = o * lax.rsqrt(jnp.mean(o * o, axis=-1, keepdims=True) + EPS) * gn_ref[...]
        y_ref[rows, :] = (yn * _silu(z_ref[rows, :].astype(f32))).astype(bf16)
        return carry

    lax.fori_loop(0, nsuper, finish, 0)
    sf_ref[...] = st_s[0]
    sb_ref[...] = st_s[1]


def gdn(qkv_all, bg_all, z_all, ya_all, conv_w, gdn_g, s0f, s0b, *, batch, seq, row_block0):
    nsuper = seq // SUPER
    nch = seq // CHUNK
    nch_pad = max(nch, SUBLANES)
    cw = jnp.zeros((3 * H_A, SUBLANES, LANES), f32).at[:, :CONV_K, :].set(
        conv_w.reshape(CONV_K, 3 * H_A, LANES).transpose(1, 0, 2))

    col = lambda off: pl.BlockSpec((seq, LANES), lambda b, h: (row_block0 + b, off + h))
    cwspec = lambda off: pl.BlockSpec((None, SUBLANES, LANES), lambda b, h: (off + h, 0, 0))
    st_spec = pl.BlockSpec((None, None, DK_A, DV_A), lambda b, h: (b, h, 0, 0))
    kern = functools.partial(_gdn_kernel, seq=seq, nsuper=nsuper)
    vmem = (2 * seq * LANES * (3 * 2 + 2 + 4 + 2) + seq * LANES * (6 * 2 + 6 * 2 + 4)
            + 4 * 4 * DK_A * DV_A * 4 + 20 * 1024 * 1024)
    return pl.pallas_call(
        kern,
        out_shape=(
            jax.ShapeDtypeStruct(ya_all.shape, bf16),
            jax.ShapeDtypeStruct((batch, H_A, DK_A, DV_A), f32),
            jax.ShapeDtypeStruct((batch, H_A, DK_A, DV_A), f32),
        ),
        grid=(batch, H_A),
        in_specs=[
            col(0), col(H_A), col(2 * H_A),
            pl.BlockSpec((seq, LANES), lambda b, h: (row_block0 + b, 0)),
            col(0),
            cwspec(0), cwspec(H_A), cwspec(2 * H_A),
            pl.BlockSpec((1, DV_A), lambda b, h: (0, 0)),
            st_spec, st_spec,
            pl.BlockSpec(memory_space=pl.ANY),
        ],
        out_specs=(col(0), st_spec, st_spec),
        scratch_shapes=[
            pltpu.VMEM((2, seq, LANES), bf16),
            pltpu.VMEM((2, seq, LANES), bf16),
            pltpu.VMEM((2, seq, LANES), bf16),
            pltpu.VMEM((2, nch, CHUNK + DK_A, CHUNK), bf16),
            pltpu.VMEM((2, nch_pad, LANES), f32),
            pltpu.VMEM((seq, LANES), f32),
            pltpu.VMEM((2, DK_A, DV_A), f32),
        ],
        input_output_aliases={11: 0},
        compiler_params=pltpu.CompilerParams(
            dimension_semantics=("parallel", "parallel"), vmem_limit_bytes=_vmem_limit(vmem)),
        name="gdn",
    )(qkv_all, qkv_all, qkv_all, bg_all, z_all, cw, cw, cw, gdn_g.reshape(1, DV_A), s0f, s0b, ya_all)


def _attn_kernel(sink_ref, q_ref, kp_ref, kc_ref, kn_ref, kx_ref, o_ref, *, n_lat_blocks, blocks_per_seq):
    i = pl.program_id(0)
    is_lat = i < n_lat_blocks
    n = i % blocks_per_seq
    kv = jnp.concatenate([kp_ref[...], kc_ref[...], kn_ref[...], kx_ref[...]], axis=0)
    k2 = kv[:, 0:LANES]
    v2 = kv[:, LANES:2 * LANES]
    nctx = kx_ref.shape[0]

    row = lax.broadcasted_iota(jnp.int32, (BLK, BLK), 0)
    col = lax.broadcasted_iota(jnp.int32, (BLK, BLK), 1)
    ok_prev = (col >= row) & is_lat & (n > 0)
    ok_cent = jnp.broadcast_to(is_lat, (BLK, BLK))
    ok_next = (col <= row) & is_lat & (n < blocks_per_seq - 1)
    zeros = jnp.zeros((BLK, BLK), f32)
    bias = jnp.concatenate(
        [jnp.where(ok_prev, zeros, NEG_BIG), jnp.where(ok_cent, zeros, NEG_BIG),
         jnp.where(ok_next, zeros, NEG_BIG), jnp.zeros((BLK, nctx), f32)], axis=1)
    bias2 = jnp.concatenate([bias, bias], axis=0)

    lane = lax.broadcasted_iota(jnp.int32, (BLK, LANES), 1)
    lo = lane < DH_B
    lo_bf = jnp.where(lo, 1.0, 0.0).astype(bf16)
    hi_bf = jnp.where(lo, 0.0, 1.0).astype(bf16)
    rows2 = lax.broadcasted_iota(jnp.int32, (2 * BLK, 1), 0)
    for p in range(HQ_B // 2):
        q2 = q_ref[:, p * LANES:(p + 1) * LANES]
        qs = jnp.concatenate([q2 * lo_bf, q2 * hi_bf], axis=0)
        s = _dot_nt(qs, k2) + bias2
        sink = jnp.where(rows2 < BLK, sink_ref[p], sink_ref[HQ_B // 2 + p])
        m = jnp.maximum(jnp.max(s, axis=-1, keepdims=True), sink)
        e = jnp.exp(s - m)
        den = jnp.sum(e, axis=-1, keepdims=True) + jnp.exp(sink - m)
        o = _dot(e.astype(bf16), v2) * (1.0 / den)
        o_ref[:, p * LANES:(p + 1) * LANES] = jnp.where(lo, o[0:BLK], o[BLK:2 * BLK]).astype(bf16)


def attention(qb_all, kv_all, sink, *, batch, seq, ctx_len, with_ctx_queries):
    t_lat = batch * seq
    bps = seq // BLK
    n_lat = t_lat // BLK
    cps = ctx_len // BLK
    n_blocks = n_lat + (batch * cps if with_ctx_queries else 0)
    ctx_blk0 = t_lat // ctx_len

    def bidx(i):
        return jnp.where(i < n_lat, i // bps, (i - n_lat) // cps)

    def nbr(delta):
        def f(i):
            n = i % bps
            j = jnp.clip(n + delta, 0, bps - 1)
            return jnp.where(i < n_lat, (i // bps) * bps + j, 0)
        return f

    kern = functools.partial(_attn_kernel, n_lat_blocks=n_lat, blocks_per_seq=bps)
    return pl.pallas_call(
        kern,
        out_shape=jax.ShapeDtypeStruct(qb_all.shape, bf16),
        grid=(n_blocks,),
        in_specs=[
            pl.BlockSpec(memory_space=pltpu.SMEM),
            pl.BlockSpec((BLK, W_QB), lambda i: (i, 0)),
            pl.BlockSpec((BLK, W_KV), lambda i: (nbr(-1)(i), 0)),
            pl.BlockSpec((BLK, W_KV), lambda i: (nbr(0)(i), 0)),
            pl.BlockSpec((BLK, W_KV), lambda i: (nbr(1)(i), 0)),
            pl.BlockSpec((ctx_len, W_KV), lambda i: (ctx_blk0 + bidx(i), 0)),
        ],
        out_specs=pl.BlockSpec((BLK, W_QB), lambda i: (i, 0)),
        compiler_params=pltpu.CompilerParams(
            dimension_semantics=("parallel",), vmem_limit_bytes=_vmem_limit(32 * 1024 * 1024)),
        name="attn",
    )(sink, qb_all, kv_all, kv_all, kv_all, kv_all)


def _merge_kernel(x_ref, ya_ref, yb_ref, gate_ref, woa_ref, wob_ref, wo_ref, g1_ref, n2_ref, sh2_ref, sc2_ref,
                  wrh_ref, wrl_ref, rb_ref, x1_ref, hp_ref, e12_ref, wk_ref):
    pa = _dot(ya_ref[...], woa_ref[...])
    pb = _dot(yb_ref[...], wob_ref[...])
    m = gate_ref[:, 0:D_MODEL].astype(f32) * pa + gate_ref[:, D_MODEL:2 * D_MODEL].astype(f32) * pb
    y = _dot(m.astype(bf16), wo_ref[...])
    x1 = x_ref[...] + g1_ref[...] * y
    x1_ref[...] = x1
    ms = jnp.mean(x1 * x1, axis=-1, keepdims=True)
    h2 = (x1 * lax.rsqrt(ms + EPS) * n2_ref[...]) * (1.0 + sc2_ref[...]) + sh2_ref[...]
    h_hi = h2.astype(bf16)
    hp_ref[...] = _pack_halves(h_hi.astype(f32))
    h_lo = (h2 - h_hi.astype(f32)).astype(bf16)

    logit = _dot_nt(wrh_ref[...], h_hi) + _dot_nt(wrl_ref[...], h_hi) + _dot_nt(wrh_ref[...], h_lo)
    score = _sigmoid(logit)
    sel = score + rb_ref[...]
    tm = sel.shape[1]

    def top2_in_group(g):
        r = [sel[g * EXP_PER_GROUP + j:g * EXP_PER_GROUP + j + 1, :] for j in range(EXP_PER_GROUP)]
        pair = None
        for a in range(EXP_PER_GROUP):
            for b in range(a + 1, EXP_PER_GROUP):
                s_ab = r[a] + r[b]
                pair = s_ab if pair is None else jnp.maximum(pair, s_ab)
        best = r[0]
        i1 = jnp.zeros((1, tm), jnp.int32)
        for j in range(1, EXP_PER_GROUP):
            better = r[j] > best
            best = jnp.where(better, r[j], best)
            i1 = jnp.where(better, j, i1)
        second = jnp.full((1, tm), -jnp.inf, f32)
        i2 = jnp.zeros((1, tm), jnp.int32)
        for j in range(EXP_PER_GROUP):
            cand = jnp.where(i1 == j, -jnp.inf, r[j])
            better = cand > second
            second = jnp.where(better, cand, second)
            i2 = jnp.where(better, j, i2)
        return pair, i1 + g * EXP_PER_GROUP, i2 + g * EXP_PER_GROUP

    gs, e1, e2 = top2_in_group(0)
    for g in range(1, N_GROUPS):
        gs_g, e1_g, e2_g = top2_in_group(g)
        better = gs_g > gs
        gs = jnp.where(better, gs_g, gs)
        e1 = jnp.where(better, e1_g, e1)
        e2 = jnp.where(better, e2_g, e2)

    eidx = lax.broadcasted_iota(jnp.int32, (N_EXPERTS, tm), 0)
    hit1 = eidx == e1
    hit2 = eidx == e2
    s1 = jnp.sum(jnp.where(hit1, score, 0.0), axis=0, keepdims=True)
    s2 = jnp.sum(jnp.where(hit2, score, 0.0), axis=0, keepdims=True)
    tot = s1 + s2
    e12_ref[...] = jnp.concatenate([e1, e2], axis=0)
    wk_pad = jnp.concatenate([s1 / tot, s2 / tot, jnp.zeros((LANES - 2, tm), f32)], axis=0)
    wk_ref[...] = wk_pad.T


def merge(x_all, ya_all, yb_all, gate_all, w_oa, w_ob, w_o, mod_l, norm2_g, wr_hi, wr_lo, router_bias,
          *, n_rows, n_lat_tiles_per_seq, n_lat_tiles, batch, tm):
    n_tiles = n_rows // tm

    def seg(i):
        return jnp.where(i < n_lat_tiles, i // n_lat_tiles_per_seq, batch)

    row = lambda w: pl.BlockSpec((tm, w), lambda i: (i, 0))
    full = lambda a, b: pl.BlockSpec((a, b), lambda i: (0, 0))
    modspec = lambda j: pl.BlockSpec((None, 1, D_MODEL), lambda i: (seg(i), 0, j))
    vmem = 2 * (2 * W_Z * D_MODEL * 2 + D_MODEL * D_MODEL * 2) + 2 * tm * (
        D_MODEL * 4 * 2 + 2 * W_Z * 2 + W_GATE * 2 + D_MODEL * 2 + LANES * 4) + 8 * tm * D_MODEL * 4
    return pl.pallas_call(
        _merge_kernel,
        out_shape=(
            jax.ShapeDtypeStruct((n_rows, D_MODEL), f32),
            jax.ShapeDtypeStruct((n_rows, D_MODEL // 2), jnp.uint32),
            jax.ShapeDtypeStruct((2, n_rows), jnp.int32),
            jax.ShapeDtypeStruct((n_rows, LANES), f32),
        ),
        grid=(n_tiles,),
        in_specs=[
            row(D_MODEL), row(W_Z), row(W_QB), row(W_GATE),
            full(W_Z, D_MODEL), full(W_QB, D_MODEL), full(D_MODEL, D_MODEL),
            modspec(2), full(1, D_MODEL), modspec(3), modspec(4),
            full(N_EXPERTS, D_MODEL), full(N_EXPERTS, D_MODEL), full(N_EXPERTS, 1),
        ],
        out_specs=(row(D_MODEL), row(D_MODEL // 2), pl.BlockSpec((2, tm), lambda i: (0, i)), row(LANES)),
        compiler_params=pltpu.CompilerParams(
            dimension_semantics=("parallel",), vmem_limit_bytes=_vmem_limit(vmem)),
        name="merge",
    )(x_all, ya_all, yb_all, gate_all, w_oa, w_ob, w_o, mod_l, norm2_g.reshape(1, D_MODEL), mod_l, mod_l,
      wr_hi, wr_lo, router_bias.reshape(N_EXPERTS, 1))


def _moe_kernel(ord_ref, tok_ref, st_ref, x1_ref, hp_ref, wk_ref, wg_ref, wu_ref, wd_ref, g2_ref, fg_ref, o_ref,
                slot_s, xg_s, yp_s, *, tm, final_norm):
    e = pl.program_id(1)
    half = D_MODEL // 2
    start = st_ref[0, e]
    count = st_ref[0, e + 1] - start

    @pl.when(e == 0)
    def _():
        xg_s[...] = jnp.zeros_like(xg_s)

    def chunk(c, carry):
        base = start + c * MOE_CHUNK
        ngroups = (jnp.minimum(count - c * MOE_CHUNK, MOE_CHUNK) + SUBLANES - 1) // SUBLANES

        def gather(g, carry2):
            for u in range(SUBLANES):
                r = g * SUBLANES + u
                xg_s[pl.ds(r, 1), :] = hp_ref[pl.ds(tok_ref[0, base + r], 1), :]
            return carry2

        lax.fori_loop(0, ngroups, gather, 0)
        x_lo, x_hi = _unpack_halves(xg_s[...])
        x_lo = x_lo.astype(bf16)
        x_hi = x_hi.astype(bf16)
        a = _dot(x_lo, wg_ref[0:half, :]) + _dot(x_hi, wg_ref[half:D_MODEL, :])
        b = _dot(x_lo, wu_ref[0:half, :]) + _dot(x_hi, wu_ref[half:D_MODEL, :])
        he = (_silu(a) * b).astype(bf16)
        y = _dot(he, wd_ref[...])
        yp_s[...] = _pack_halves(y)

        def scatter(g, carry2):
            for u in range(SUBLANES):
                r = g * SUBLANES + u
                slot_s[pl.ds(ord_ref[0, base + r], 1), :] = yp_s[pl.ds(r, 1), :]
            return carry2

        lax.fori_loop(0, ngroups, scatter, 0)
        return carry

    lax.fori_loop(0, (count + MOE_CHUNK - 1) // MOE_CHUNK, chunk, 0)

    @pl.when(e == N_EXPERTS - 1)
    def _():
        sub = 256
        lane = lax.broadcasted_iota(jnp.int32, (sub, LANES), 1)
        for r in range(tm // sub):
            rows = pl.ds(r * sub, sub)
            wk = wk_ref[rows, :]
            w1 = jnp.sum(jnp.where(lane == 0, wk, 0.0), axis=-1, keepdims=True)
            w2 = jnp.sum(jnp.where(lane == 1, wk, 0.0), axis=-1, keepdims=True)
            lo1, hi1 = _unpack_halves(slot_s[pl.ds(r * sub, sub), :])
            lo2, hi2 = _unpack_halves(slot_s[pl.ds(tm + r * sub, sub), :])
            acc = jnp.concatenate([w1 * lo1 + w2 * lo2, w1 * hi1 + w2 * hi2], axis=1)
            x2 = x1_ref[rows, :] + g2_ref[...] * acc
            if final_norm:
                ms = jnp.mean(x2 * x2, axis=-1, keepdims=True)
                x2 = x2 * lax.rsqrt(ms + EPS) * fg_ref[...]
            o_ref[rows, :] = x2


def moe(x1, hp, e12, wk, w_gate, w_up, w_down, mod_l, final_g, *, n_lat_tiles_per_seq, n_lat_tiles, batch, tm,
        final_norm):
    n_rows = x1.shape[0]
    n_tiles = n_rows // tm

    ids = e12.reshape(2, n_tiles, tm).transpose(1, 0, 2).reshape(n_tiles, 2 * tm)
    order = jnp.argsort(ids, axis=1, stable=True).astype(jnp.int32)
    tok = jnp.pad(jnp.where(order >= tm, order - tm, order), ((0, 0), (0, MOE_CHUNK)))
    order = jnp.pad(order, ((0, 0), (0, MOE_CHUNK)), constant_values=2 * tm)
    counts = jnp.sum((ids[:, :, None] == jnp.arange(N_EXPERTS, dtype=jnp.int32)[None, None, :]).astype(jnp.int32),
                     axis=1)
    starts = jnp.concatenate([jnp.zeros((n_tiles, 1), jnp.int32), jnp.cumsum(counts, axis=1)], axis=1)
    starts = jnp.pad(starts, ((0, 0), (0, 2 * N_EXPERTS - (N_EXPERTS + 1))))

    def seg(i):
        return jnp.where(i < n_lat_tiles, i // n_lat_tiles_per_seq, batch)

    row = lambda w: pl.BlockSpec((tm, w), lambda i, e: (i, 0))
    smem_row = lambda w: pl.BlockSpec((None, 1, w), lambda i, e: (i, 0, 0), memory_space=pltpu.SMEM)
    kern = functools.partial(_moe_kernel, tm=tm, final_norm=final_norm)
    vmem = (2 * 3 * D_MODEL * D_EXPERT * 2 + 2 * tm * (D_MODEL * 4 * 2 + D_MODEL * 2 + LANES * 4)
            + 2 * tm * D_MODEL * 2 + 16 * 1024 * 1024)
    return pl.pallas_call(
        kern,
        out_shape=jax.ShapeDtypeStruct((n_rows, D_MODEL), f32),
        grid=(n_tiles, N_EXPERTS),
        in_specs=[
            smem_row(2 * tm + MOE_CHUNK), smem_row(2 * tm + MOE_CHUNK), smem_row(2 * N_EXPERTS),
            row(D_MODEL), row(D_MODEL // 2), row(LANES),
            pl.BlockSpec((None, D_MODEL, D_EXPERT), lambda i, e: (e, 0, 0)),
            pl.BlockSpec((None, D_MODEL, D_EXPERT), lambda i, e: (e, 0, 0)),
            pl.BlockSpec((None, D_EXPERT, D_MODEL), lambda i, e: (e, 0, 0)),
            pl.BlockSpec((None, 1, D_MODEL), lambda i, e: (seg(i), 0, 5)),
            pl.BlockSpec((1, D_MODEL), lambda i, e: (0, 0)),
        ],
        out_specs=row(D_MODEL),
        scratch_shapes=[
            pltpu.VMEM((2 * tm + SUBLANES, D_MODEL // 2), jnp.uint32),
            pltpu.VMEM((MOE_CHUNK, D_MODEL // 2), jnp.uint32),
            pltpu.VMEM((MOE_CHUNK, D_MODEL // 2), jnp.uint32),
        ],
        compiler_params=pltpu.CompilerParams(
            dimension_semantics=("parallel", "arbitrary"), vmem_limit_bytes=_vmem_limit(vmem)),
        name="moe",
    )(order.reshape(n_tiles, 1, 2 * tm + MOE_CHUNK), tok.reshape(n_tiles, 1, 2 * tm + MOE_CHUNK),
      starts.reshape(n_tiles, 1, 2 * N_EXPERTS), x1, hp, wk,
      w_gate, w_up, w_down, mod_l, final_g.reshape(1, D_MODEL))


def _prep_w_in(w_in_l):
    qkv_a = 3 * H_A * DK_A
    o_z = qkv_a
    o_b = o_z + H_A * DV_A
    o_a = o_b + 2 * H_A
    o_q = o_a + 2 * H_A
    o_k = o_q + HQ_B * DH_B
    o_v = o_k + HKV_B * DH_B
    o_ga = o_v + HKV_B * DH_B
    w_bg = jnp.zeros((D_MODEL, W_BG), f32).at[:, :4 * H_A].set(w_in_l[:, o_b:o_q])
    wq = w_in_l[:, o_q:o_k].reshape(D_MODEL, HQ_B, DH_B)[:, np.array(HEAD_PERM), :].reshape(D_MODEL, W_QB)
    wq = wq * (DH_B ** -0.5)
    cat = jnp.concatenate(
        [w_in_l[:, :qkv_a], w_in_l[:, o_z:o_b], w_bg, wq, w_in_l[:, o_k:o_v], w_in_l[:, o_v:o_ga],
         w_in_l[:, o_ga:]], axis=1)
    return cat.astype(bf16)


def _rope_tables(seq, tm):
    t = np.arange(seq)
    rows = (t // GRID_W).astype(np.float32)
    cols = (t % GRID_W).astype(np.float32)
    nf = DH_B // 4
    inv = jnp.asarray(ROPE_THETA, f32) ** (-jnp.arange(nf, dtype=f32) / nf)
    ang_r = jnp.asarray(rows)[:, None] * inv[None, :]
    ang_c = jnp.asarray(cols)[:, None] * inv[None, :]
    cos64 = jnp.concatenate([jnp.cos(ang_r), jnp.cos(ang_r), jnp.cos(ang_c), jnp.cos(ang_c)], axis=1)
    zero = jnp.zeros_like(ang_r)
    sa64 = jnp.concatenate([-jnp.sin(ang_r), zero, -jnp.sin(ang_c), zero], axis=1)
    sb64 = jnp.concatenate([zero, jnp.sin(ang_r), zero, jnp.sin(ang_c)], axis=1)

    def fin(tab, fill):
        tab = jnp.concatenate([tab, tab], axis=1)
        return jnp.concatenate([tab, jnp.full((tm, LANES), fill, f32)], axis=0)

    return fin(cos64, 1.0), fin(sa64, 0.0), fin(sb64, 0.0)


TM_TOKEN = 256
TM_MOE = 1024


def kernel(x, c, ctx, c_ctx, w_mod, b_mod, norm1_g, norm2_g, w_in, conv_w, a_log, dt_bias, gdn_norm_g, sink,
           w_oa, w_ob, w_o, w_router, router_bias, w_gate, w_up, w_down, final_g):
    batch, seq, d = x.shape
    ctx_len = ctx.shape[1]
    depth = w_mod.shape[0]
    assert d == D_MODEL and seq % SUPER == 0 and ctx_len % SUPER == 0 and seq % TM_MOE == 0
    t_lat = batch * seq
    t_ctx = batch * ctx_len
    assert t_ctx % TM_MOE == 0

    x_all = jnp.concatenate([x.reshape(t_lat, d), ctx.reshape(t_ctx, d)], axis=0)
    t_all = t_lat + t_ctx

    n_cond = ((batch + 1 + SUBLANES - 1) // SUBLANES) * SUBLANES
    cond = jnp.zeros((n_cond, d), f32).at[:batch].set(c).at[batch].set(c_ctx)
    mod = adaln_all(cond, w_mod, b_mod).reshape(depth, n_cond, 1, 6 * d)

    rope_tabs = _rope_tables(seq, TM_TOKEN)
    wr_hi = w_router.T.astype(bf16)
    wr_lo = (w_router.T - wr_hi.astype(f32)).astype(bf16)
    perm_rows = np.concatenate([np.arange(h * DH_B, (h + 1) * DH_B) for h in HEAD_PERM])
    zero_state = jnp.zeros((batch, H_A, DK_A, DV_A), f32)

    tok = dict(n_lat_tiles_per_seq=seq // TM_TOKEN, n_lat_tiles=t_lat // TM_TOKEN, batch=batch, tm=TM_TOKEN)
    tok_moe = dict(n_lat_tiles_per_seq=seq // TM_MOE, n_lat_tiles=t_lat // TM_MOE, batch=batch, tm=TM_MOE)

    for l in range(depth):
        need_ctx = l < depth - 1
        w_cat = _prep_w_in(w_in[l])
        qkv_all, z_all, bg_all, qb_all, kv_all, gate_all = inproj(
            x_all, mod[l], norm1_g[l], w_cat, rope_tabs, a_log[l], dt_bias[l], **tok)

        ya_all = jnp.zeros((t_all, W_Z), bf16)
        ya_all, s_f, s_b = gdn(qkv_all, bg_all, z_all, ya_all, conv_w[l], gdn_norm_g[l], zero_state, zero_state,
                               batch=batch, seq=ctx_len, row_block0=t_lat // ctx_len)
        ya_all, _, _ = gdn(qkv_all, bg_all, z_all, ya_all, conv_w[l], gdn_norm_g[l], s_f, s_b,
                           batch=batch, seq=seq, row_block0=0)

        yb_all = attention(qb_all, kv_all, sink[l], batch=batch, seq=seq, ctx_len=ctx_len,
                           with_ctx_queries=need_ctx)

        n_rows = t_all if need_ctx else t_lat
        x1, hp, e12, wk = merge(
            x_all, ya_all, yb_all, gate_all, w_oa[l].astype(bf16), w_ob[l][perm_rows].astype(bf16),
            w_o[l].astype(bf16), mod[l], norm2_g[l], wr_hi, wr_lo, router_bias, n_rows=n_rows, **tok)
        x_all = moe(x1, hp, e12, wk, w_gate[l].astype(bf16), w_up[l].astype(bf16), w_down[l].astype(bf16),
                    mod[l], final_g, final_norm=not need_ctx, **tok_moe)

    return x_all[:t_lat].reshape(batch, seq, d)
```

```python
import functools

import jax
import jax.numpy as jnp
import numpy as np
from jax import lax
from jax.experimental import pallas as pl
from jax.experimental.pallas import tpu as pltpu

f32 = jnp.float32
bf16 = jnp.bfloat16

D_MODEL = 1024
GRID_W = 64
EPS = 1e-6
H_A = 4
DK_A = 128
DV_A = 128
CONV_K = 5
CHUNK = 64
HQ_B = 8
HKV_B = 2
DH_B = 64
WINDOW = 128
BLK = 128
ROPE_THETA = 10000.0
N_EXPERTS = 16
N_GROUPS = 4
EXP_PER_GROUP = 4
D_EXPERT = 512

LANES = 128
SUBLANES = 8
BF16_ROWS = 16
VMEM_BYTES_V7X = 64 * 1024 * 1024

W_QKV = 3 * H_A * DK_A
W_Z = H_A * DV_A
W_BG = LANES
W_QB = HQ_B * DH_B
W_KV = 2 * HKV_B * DH_B
W_GATE = 2 * D_MODEL
OFF_Z = W_QKV
OFF_BG = OFF_Z + W_Z
OFF_QB = OFF_BG + W_BG
OFF_KV = OFF_QB + W_QB
OFF_GATE = OFF_KV + W_KV
W_TOTAL = OFF_GATE + W_GATE
W_ROPE = W_QB + HKV_B * DH_B

NEG_BIG = -1e30
HEAD_PERM = (0, 4, 1, 5, 2, 6, 3, 7)

SUPER = 4 * CHUNK
HALF = CHUNK // 2
MOE_CHUNK = 192


def _vmem_limit(nbytes):
    return int(min(max(nbytes, 16 * 1024 * 1024), VMEM_BYTES_V7X - 8 * 1024 * 1024))


def _split3(x):
    hi = x.astype(bf16)
    r1 = x - hi.astype(f32)
    mid = r1.astype(bf16)
    lo = (r1 - mid.astype(f32)).astype(bf16)
    return hi, mid, lo


def _dot(a, b):
    return jnp.dot(a, b, preferred_element_type=f32)


def _dot_nt(a, b):
    return lax.dot_general(a, b, (((1,), (1,)), ((), ())), preferred_element_type=f32)


def _dot_exact_lhs_mask(mask_bf, parts):
    return _dot(mask_bf, parts[0]) + _dot(mask_bf, parts[1]) + _dot(mask_bf, parts[2])


def _pack_halves(x):
    w = x.shape[1] // 2
    lo = lax.bitcast_convert_type(x[:, 0:w].astype(bf16).astype(f32), jnp.uint32)
    hi = lax.bitcast_convert_type(x[:, w:2 * w].astype(bf16).astype(f32), jnp.uint32)
    return lax.shift_right_logical(lo, jnp.uint32(16)) | (hi & jnp.uint32(0xFFFF0000))


def _unpack_halves(p):
    lo = lax.bitcast_convert_type(lax.shift_left(p, jnp.uint32(16)), f32)
    hi = lax.bitcast_convert_type(p & jnp.uint32(0xFFFF0000), f32)
    return lo, hi


def _sigmoid(x):
    return 1.0 / (1.0 + jnp.exp(-x))


def _silu(x):
    return x * _sigmoid(x)


def _softplus(x):
    return jnp.maximum(x, 0.0) + jnp.log1p(jnp.exp(-jnp.abs(x)))


def _adaln_kernel(c_ref, w_ref, b_ref, o_ref):
    c = c_ref[...]
    s = _silu(c).astype(bf16)
    o_ref[...] = _dot(s, w_ref[...].astype(bf16)) + b_ref[...]


def adaln_all(cond, w_mod, b_mod):
    depth = w_mod.shape[0]
    r = cond.shape[0]
    nblk = 6
    return pl.pallas_call(
        _adaln_kernel,
        out_shape=jax.ShapeDtypeStruct((depth, r, 6 * D_MODEL), f32),
        grid=(depth, nblk),
        in_specs=[
            pl.BlockSpec((r, D_MODEL), lambda l, j: (0, 0)),
            pl.BlockSpec((None, D_MODEL, D_MODEL), lambda l, j: (l, 0, j)),
            pl.BlockSpec((None, 1, D_MODEL), lambda l, j: (l, 0, j)),
        ],
        out_specs=pl.BlockSpec((None, r, D_MODEL), lambda l, j: (l, 0, j)),
        compiler_params=pltpu.CompilerParams(
            dimension_semantics=("parallel", "parallel"),
            vmem_limit_bytes=_vmem_limit(24 * 1024 * 1024)),
        name="adaln",
    )(cond, w_mod, b_mod.reshape(depth, 1, 6 * D_MODEL))


def _inproj_kernel(x_ref, g_ref, sh_ref, sc_ref, cos_ref, sa_ref, sb_ref, alog_ref, dtb_ref, w_ref,
                   qkv_ref, z_ref, bg_ref, qb_ref, kv_ref, gate_ref):
    x = x_ref[...]
    ms = jnp.mean(x * x, axis=-1, keepdims=True)
    y = x * lax.rsqrt(ms + EPS) * g_ref[...]
    h = (y * (1.0 + sc_ref[...]) + sh_ref[...]).astype(bf16)

    qkv_ref[...] = _dot(h, w_ref[:, 0:W_QKV]).astype(bf16)
    z_ref[...] = _dot(h, w_ref[:, OFF_Z:OFF_Z + W_Z]).astype(bf16)
    raw = _dot(h, w_ref[:, OFF_BG:OFF_BG + W_BG])
    lane = lax.broadcasted_iota(jnp.int32, raw.shape, 1)
    bg_ref[...] = jnp.where(lane < 2 * H_A, _sigmoid(raw),
                            -jnp.exp(alog_ref[...]) * _softplus(raw + dtb_ref[...]))

    pr = _dot(h, w_ref[:, OFF_QB:OFF_QB + W_ROPE])
    reps = W_ROPE // LANES
    cos = jnp.concatenate([cos_ref[...]] * reps, axis=1)
    sa = jnp.concatenate([sa_ref[...]] * reps, axis=1)
    sb = jnp.concatenate([sb_ref[...]] * reps, axis=1)
    half = DH_B // 4
    rot = pr * cos + pltpu.roll(pr, W_ROPE - half, 1) * sa + pltpu.roll(pr, half, 1) * sb
    qb_ref[...] = rot[:, 0:W_QB].astype(bf16)
    kv_ref[:, 0:LANES] = rot[:, W_QB:W_ROPE].astype(bf16)
    kv_ref[:, LANES:2 * LANES] = _dot(h, w_ref[:, OFF_KV + LANES:OFF_KV + 2 * LANES]).astype(bf16)

    gate_ref[...] = _sigmoid(_dot(h, w_ref[:, OFF_GATE:OFF_GATE + W_GATE])).astype(bf16)


def inproj(x_all, mod_l, norm_g, w_cat, rope_tabs, a_log_l, dt_bias_l, *, n_lat_tiles_per_seq, n_lat_tiles, batch,
           tm):
    t_all = x_all.shape[0]
    n_tiles = t_all // tm
    cos_t, sa_t, sb_t = rope_tabs
    pad_lanes = lambda v: jnp.zeros((1, LANES), f32).at[0, 2 * H_A:4 * H_A].set(v.reshape(-1))
    alog_v = pad_lanes(a_log_l)
    dtb_v = pad_lanes(dt_bias_l)

    def seg(i):
        return jnp.where(i < n_lat_tiles, i // n_lat_tiles_per_seq, batch)

    def pos(i):
        return jnp.where(i < n_lat_tiles, i % n_lat_tiles_per_seq, n_lat_tiles_per_seq)

    row = lambda w: pl.BlockSpec((tm, w), lambda i: (i, 0))
    tab = pl.BlockSpec((tm, LANES), lambda i: (pos(i), 0))
    out_shapes = (
        jax.ShapeDtypeStruct((t_all, W_QKV), bf16),
        jax.ShapeDtypeStruct((t_all, W_Z), bf16),
        jax.ShapeDtypeStruct((t_all, W_BG), f32),
        jax.ShapeDtypeStruct((t_all, W_QB), bf16),
        jax.ShapeDtypeStruct((t_all, W_KV), bf16),
        jax.ShapeDtypeStruct((t_all, W_GATE), bf16),
    )
    vmem = 2 * (W_TOTAL * D_MODEL * 2) + 2 * tm * (D_MODEL * 4 + W_TOTAL * 2 + 3 * LANES * 4) + tm * W_TOTAL * 4
    return pl.pallas_call(
        _inproj_kernel,
        out_shape=out_shapes,
        grid=(n_tiles,),
        in_specs=[
            row(D_MODEL),
            pl.BlockSpec((1, D_MODEL), lambda i: (0, 0)),
            pl.BlockSpec((None, 1, D_MODEL), lambda i: (seg(i), 0, 0)),
            pl.BlockSpec((None, 1, D_MODEL), lambda i: (seg(i), 0, 1)),
            tab, tab, tab,
            pl.BlockSpec((1, LANES), lambda i: (0, 0)),
            pl.BlockSpec((1, LANES), lambda i: (0, 0)),
            pl.BlockSpec((D_MODEL, W_TOTAL), lambda i: (0, 0)),
        ],
        out_specs=(row(W_QKV), row(W_Z), row(W_BG), row(W_QB), row(W_KV), row(W_GATE)),
        compiler_params=pltpu.CompilerParams(
            dimension_semantics=("parallel",), vmem_limit_bytes=_vmem_limit(vmem)),
        name="inproj",
    )(x_all, norm_g.reshape(1, D_MODEL), mod_l, mod_l, cos_t, sa_t, sb_t, alog_v, dtb_v, w_cat)


def _gdn_kernel(q_ref, k_ref, v_ref, bg_ref, z_ref, cwq_ref, cwk_ref, cwv_ref, gn_ref, s0f_ref, s0b_ref,
                yin_ref, y_ref, sf_ref, sb_ref,
                u_s, w_s, qg_s, ak_s, gl_s, o_s, st_s, a2_s, x_s, *, seq, nsuper):
    del yin_ref
    hd = pl.program_id(1)
    nch = seq // CHUNK
    cps = SUPER // CHUNK

    ii = lax.broadcasted_iota(jnp.int32, (SUPER, SUPER), 0)
    jj = lax.broadcasted_iota(jnp.int32, (SUPER, SUPER), 1)
    same = (ii // CHUNK) == (jj // CHUNK)
    tri = (same & (jj <= ii), same & (jj >= ii))
    tri_s = (same & (jj < ii), same & (jj > ii))
    cum2_bf = jnp.concatenate([jnp.where(m, 1.0, 0.0) for m in tri], axis=0).astype(bf16)
    lane = lax.broadcasted_iota(jnp.int32, (SUPER, LANES), 1)

    def conv_silu(ref, cw_ref, base, st):
        main = ref[pl.ds(base, SUPER), :].astype(f32)
        p0 = pl.multiple_of(jnp.maximum(base - BF16_ROWS, 0), BF16_ROWS)
        n0 = pl.multiple_of(jnp.minimum(base + SUPER, seq - BF16_ROWS), BF16_ROWS)
        prev = ref[pl.ds(p0, BF16_ROWS), :].astype(f32) * jnp.where(st > 0, 1.0, 0.0)
        nxt = ref[pl.ds(n0, BF16_ROWS), :].astype(f32) * jnp.where(st < nsuper - 1, 1.0, 0.0)
        win = jnp.concatenate([prev, main, nxt], axis=0)
        acc = jnp.zeros((SUPER, LANES), f32)
        for i in range(CONV_K):
            lo_row = BF16_ROWS - CONV_K // 2 + i
            acc = acc + win[lo_row:lo_row + SUPER, :] * cw_ref[i:i + 1, :]
        return _silu(acc)

    def column(x, c):
        col = jnp.sum(jnp.where(lane == c, x, 0.0), axis=-1, keepdims=True)
        return jnp.broadcast_to(col, (SUPER, LANES))

    def front(st):
        base = pl.multiple_of(st * SUPER, SUPER)
        q = conv_silu(q_ref, cwq_ref, base, st)
        yield
        k = conv_silu(k_ref, cwk_ref, base, st)
        yield
        v = conv_silu(v_ref, cwv_ref, base, st)
        yield
        q = q * lax.rsqrt(jnp.sum(q * q, axis=-1, keepdims=True) + EPS) * (DK_A ** -0.5)
        k = k * lax.rsqrt(jnp.sum(k * k, axis=-1, keepdims=True) + EPS)
        q_bf = q.astype(bf16)
        k_bf = k.astype(bf16)
        qkk = _dot_nt(jnp.concatenate([q_bf, k_bf], axis=0), k_bf)
        qk_raw = qkk[0:SUPER]
        kk_raw = qkk[SUPER:2 * SUPER]
        bg = bg_ref[pl.ds(base, SUPER), :]
        rows = pl.ds(base, SUPER)
        yield

        g_both = [column(bg, 2 * H_A + d * H_A + hd) for d in range(2)]
        parts = [jnp.concatenate(p, axis=1) for p in zip(_split3(g_both[0]), _split3(g_both[1]))]
        csum_both = _dot_exact_lhs_mask(cum2_bf, parts)
        yield

        for d in range(2):
            beta = column(bg, d * H_A + hd)
            mask, mask_s = tri[d], tri_s[d]
            csum = csum_both[d * SUPER:(d + 1) * SUPER, d * LANES:(d + 1) * LANES]
            ct = jnp.concatenate([csum[0:LANES].T, csum[LANES:2 * LANES].T], axis=1)
            ct = jnp.concatenate([ct, ct], axis=0)
            c2 = jnp.concatenate([csum, csum], axis=1)
            decay = jnp.where(mask, jnp.exp(jnp.where(mask, c2 - ct, 0.0)), 0.0)
            yield
            beta2 = jnp.concatenate([beta, beta], axis=1)
            a_mat = jnp.where(mask_s, beta2 * kk_raw * decay, 0.0)
            if d == 1:
                a_mat = a_mat.T
            for c in range(cps):
                cs = slice(c * CHUNK, (c + 1) * CHUNK)
                blk = a_mat[cs, cs]
                prob = d * nch + st * cps + c
                a2_s[pl.ds(pl.multiple_of(prob * HALF, HALF), HALF), :] = jnp.concatenate(
                    [blk[0:HALF], blk[HALF:CHUNK]], axis=1)
            attn = jnp.where(mask, qk_raw * decay, 0.0)
            yield

            last = (CHUNK - 1, 0)[d]
            tot = jnp.concatenate(
                [jnp.broadcast_to(csum[c * CHUNK + last:c * CHUNK + last + 1, :], (CHUNK, LANES))
                 for c in range(cps)], axis=0)
            egc = jnp.exp(csum)
            u_s[d, rows, :] = (v * beta).astype(bf16)
            w_s[d, rows, :] = (k * beta * egc).astype(bf16)
            qg_s[d, rows, :] = (q * egc).astype(bf16)
            yield
            kg_t = (k * jnp.exp(tot - csum)).T
            glast = jnp.exp(tot)
            for c in range(cps):
                cs = slice(c * CHUNK, (c + 1) * CHUNK)
                ak_s[d, st * cps + c] = jnp.concatenate([attn[cs, cs], kg_t[:, cs]], axis=0).astype(bf16)
                gl_s[d, pl.ds(st * cps + c, 1), :] = glast[c * CHUNK:c * CHUNK + 1, :]
            yield

    def back(st):
        rows = pl.ds(pl.multiple_of(st * SUPER, SUPER), SUPER)
        lane2 = lax.broadcasted_iota(jnp.int32, (CHUNK, SUPER), 1)
        for d in range(2):
            blocks = []
            for c in range(cps):
                prob = d * nch + st * cps + c
                half = a2_s[pl.ds(pl.multiple_of(prob * HALF, HALF), HALF), :]
                blk = jnp.concatenate([half[:, 0:CHUNK], half[:, CHUNK:2 * CHUNK]], axis=0)
                wide = jnp.concatenate([blk] * cps, axis=1)
                blocks.append(jnp.where(lane2 // CHUNK == c, wide, 0.0))
            t_m = jnp.concatenate(blocks, axis=0)
            if d == 1:
                t_m = t_m.T
            yield
            rhs = jnp.concatenate([u_s[d, rows, :], w_s[d, rows, :]], axis=1)
            sol = rhs.astype(f32) + _dot(t_m.astype(bf16), rhs)
            u_s[d, rows, :] = sol[:, 0:DV_A].astype(bf16)
            w_s[d, rows, :] = sol[:, DV_A:2 * DV_A].astype(bf16)
            yield

    def interleave(*gens):
        active = list(gens)
        while active:
            for g in list(active):
                if next(g, StopIteration) is StopIteration:
                    active.remove(g)

    nprob = 2 * nch
    if nprob < LANES:
        a2_s[...] = jnp.zeros_like(a2_s)

    def front_loop(st, carry):
        interleave(front(st))
        return carry

    lax.fori_loop(0, nsuper, front_loop, 0)

    for r in range(HALF):
        xt = a2_s[pl.ds(r, LANES, stride=HALF), :].T
        x_s[r] = xt[0:CHUNK]
        x_s[r + HALF] = xt[CHUNK:2 * CHUNK]

    nblk = CHUNK // SUBLANES
    for i in range(1, CHUNK):
        acc = [-x_s[i, jb * SUBLANES:(jb + 1) * SUBLANES, :] for jb in range(nblk)]
        for k in range(1, i):
            a_ik = x_s[i, k:k + 1, :]
            for jb in range((k - 1) // SUBLANES + 1):
                acc[jb] = acc[jb] - a_ik * x_s[k, jb * SUBLANES:(jb + 1) * SUBLANES, :]
        for jb in range(nblk):
            x_s[i, jb * SUBLANES:(jb + 1) * SUBLANES, :] = acc[jb]

    for r in range(HALF):
        yt = jnp.concatenate([x_s[r], x_s[r + HALF]], axis=0).T
        a2_s[pl.ds(r, LANES, stride=HALF), :] = yt

    def back_loop(st, carry):
        interleave(back(st))
        return carry

    lax.fori_loop(0, nsuper, back_loop, 0)

    st_s[0] = s0f_ref[...]
    st_s[1] = s0b_ref[...]

    def scan_step(n, accumulate):
        for d in range(2):
            c = n if d == 0 else nch - 1 - n
            rows = pl.ds(pl.multiple_of(c * CHUNK, CHUNK), CHUNK)
            s_old = st_s[d]
            ws = _dot(jnp.concatenate([w_s[d, rows, :], qg_s[d, rows, :]], axis=0), s_old.astype(bf16))
            v_new = (u_s[d, rows, :].astype(f32) - ws[0:CHUNK]).astype(bf16)
            ov = _dot(ak_s[d, c], v_new)
            o = ws[CHUNK:2 * CHUNK] + ov[0:CHUNK]
            st_s[d] = s_old * gl_s[d, pl.ds(c, 1), :] + ov[CHUNK:CHUNK + DK_A]
            if accumulate:
                o_s[rows, :] = o_s[rows, :] + o
            else:
                o_s[rows, :] = o

    def scan_a(n, carry):
        scan_step(n, False)
        return carry

    def scan_b(n, carry):
        scan_step(n, True)
        return carry

    lax.fori_loop(0, nch // 2, scan_a, 0)
    lax.fori_loop(nch // 2, nch, scan_b, 0)

    def finish(st, carry):
        rows = pl.ds(pl.multiple_of(st * SUPER, SUPER), SUPER)
        o = o_s[rows, :]
        yn = o * lax.rsqrt(jnp.mean(o * o, axis=-1, keepdims=True) + EPS) * gn_ref[...]
        y_ref[rows, :] = (yn * _silu(z_ref[rows, :].astype(f32))).astype(bf16)
        return carry

    lax.fori_loop(0, nsuper, finish, 0)
    sf_ref[...] = st_s[0]
    sb_ref[...] = st_s[1]


def gdn(qkv_all, bg_all, z_all, ya_all, conv_w, gdn_g, s0f, s0b, *, batch, seq, row_block0):
    nsuper = seq // SUPER
    nch = seq // CHUNK
    nch_pad = max(nch, SUBLANES)
    cw = jnp.zeros((3 * H_A, SUBLANES, LANES), f32).at[:, :CONV_K, :].set(
        conv_w.reshape(CONV_K, 3 * H_A, LANES).transpose(1, 0, 2))

    col = lambda off: pl.BlockSpec((seq, LANES), lambda b, h: (row_block0 + b, off + h))
    cwspec = lambda off: pl.BlockSpec((None, SUBLANES, LANES), lambda b, h: (off + h, 0, 0))
    st_spec = pl.BlockSpec((None, None, DK_A, DV_A), lambda b, h: (b, h, 0, 0))
    kern = functools.partial(_gdn_kernel, seq=seq, nsuper=nsuper)
    vmem = (2 * seq * LANES * (3 * 2 + 2 + 4 + 2) + seq * LANES * (6 * 2 + 6 * 2 + 4)
            + 4 * 4 * DK_A * DV_A * 4 + 20 * 1024 * 1024)
    return pl.pallas_call(
        kern,
        out_shape=(
            jax.ShapeDtypeStruct(ya_all.shape, bf16),
            jax.ShapeDtypeStruct((batch, H_A, DK_A, DV_A), f32),
            jax.ShapeDtypeStruct((batch, H_A, DK_A, DV_A), f32),
        ),
        grid=(batch, H_A),
        in_specs=[
            col(0), col(H_A), col(2 * H_A),
            pl.BlockSpec((seq, LANES), lambda b, h: (row_block0 + b, 0)),
            col(0),
            cwspec(0), cwspec(H_A), cwspec(2 * H_A),
            pl.BlockSpec((1, DV_A), lambda b, h: (0, 0)),
            st_spec, st_spec,
            pl.BlockSpec(memory_space=pl.ANY),
        ],
        out_specs=(col(0), st_spec, st_spec),
        scratch_shapes=[
            pltpu.VMEM((2, seq, LANES), bf16),
            pltpu.VMEM((2, seq, LANES), bf16),
            pltpu.VMEM((2, seq, LANES), bf16),
            pltpu.VMEM((2, nch, CHUNK + DK_A, CHUNK), bf16),
            pltpu.VMEM((2, nch_pad, LANES), f32),
            pltpu.VMEM((seq, LANES), f32),
            pltpu.VMEM((2, DK_A, DV_A), f32),
            pltpu.VMEM((LANES * HALF, LANES), f32),
            pltpu.VMEM((CHUNK, CHUNK, LANES), f32),
        ],
        input_output_aliases={11: 0},
        compiler_params=pltpu.CompilerParams(
            dimension_semantics=("parallel", "parallel"), vmem_limit_bytes=_vmem_limit(vmem)),
        name="gdn",
    )(qkv_all, qkv_all, qkv_all, bg_all, z_all, cw, cw, cw, gdn_g.reshape(1, DV_A), s0f, s0b, ya_all)


def _attn_kernel(sink_ref, q_ref, kp_ref, kc_ref, kn_ref, kx_ref, o_ref, *, n_lat_blocks, blocks_per_seq):
    i = pl.program_id(0)
    is_lat = i < n_lat_blocks
    n = i % blocks_per_seq
    kv = jnp.concatenate([kp_ref[...], kc_ref[...], kn_ref[...], kx_ref[...]], axis=0)
    k2 = kv[:, 0:LANES]
    v2 = kv[:, LANES:2 * LANES]
    nctx = kx_ref.shape[0]

    row = lax.broadcasted_iota(jnp.int32, (BLK, BLK), 0)
    col = lax.broadcasted_iota(jnp.int32, (BLK, BLK), 1)
    ok_prev = (col >= row) & is_lat & (n > 0)
    ok_cent = jnp.broadcast_to(is_lat, (BLK, BLK))
    ok_next = (col <= row) & is_lat & (n < blocks_per_seq - 1)
    zeros = jnp.zeros((BLK, BLK), f32)
    bias = jnp.concatenate(
        [jnp.where(ok_prev, zeros, NEG_BIG), jnp.where(ok_cent, zeros, NEG_BIG),
         jnp.where(ok_next, zeros, NEG_BIG), jnp.zeros((BLK, nctx), f32)], axis=1)
    bias2 = jnp.concatenate([bias, bias], axis=0)

    lane = lax.broadcasted_iota(jnp.int32, (BLK, LANES), 1)
    lo = lane < DH_B
    lo_bf = jnp.where(lo, 1.0, 0.0).astype(bf16)
    hi_bf = jnp.where(lo, 0.0, 1.0).astype(bf16)
    rows2 = lax.broadcasted_iota(jnp.int32, (2 * BLK, 1), 0)
    for p in range(HQ_B // 2):
        q2 = q_ref[:, p * LANES:(p + 1) * LANES]
        qs = jnp.concatenate([q2 * lo_bf, q2 * hi_bf], axis=0)
        s = _dot_nt(qs, k2) + bias2
        sink = jnp.where(rows2 < BLK, sink_ref[p], sink_ref[HQ_B // 2 + p])
        m = jnp.maximum(jnp.max(s, axis=-1, keepdims=True), sink)
        e = jnp.exp(s - m)
        den = jnp.sum(e, axis=-1, keepdims=True) + jnp.exp(sink - m)
        o = _dot(e.astype(bf16), v2) * (1.0 / den)
        o_ref[:, p * LANES:(p + 1) * LANES] = jnp.where(lo, o[0:BLK], o[BLK:2 * BLK]).astype(bf16)


def attention(qb_all, kv_all, sink, *, batch, seq, ctx_len, with_ctx_queries):
    t_lat = batch * seq
    bps = seq // BLK
    n_lat = t_lat // BLK
    cps = ctx_len // BLK
    n_blocks = n_lat + (batch * cps if with_ctx_queries else 0)
    ctx_blk0 = t_lat // ctx_len

    def bidx(i):
        return jnp.where(i < n_lat, i // bps, (i - n_lat) // cps)

    def nbr(delta):
        def f(i):
            n = i % bps
            j = jnp.clip(n + delta, 0, bps - 1)
            return jnp.where(i < n_lat, (i // bps) * bps + j, 0)
        return f

    kern = functools.partial(_attn_kernel, n_lat_blocks=n_lat, blocks_per_seq=bps)
    return pl.pallas_call(
        kern,
        out_shape=jax.ShapeDtypeStruct(qb_all.shape, bf16),
        grid=(n_blocks,),
        in_specs=[
            pl.BlockSpec(memory_space=pltpu.SMEM),
            pl.BlockSpec((BLK, W_QB), lambda i: (i, 0)),
            pl.BlockSpec((BLK, W_KV), lambda i: (nbr(-1)(i), 0)),
            pl.BlockSpec((BLK, W_KV), lambda i: (nbr(0)(i), 0)),
            pl.BlockSpec((BLK, W_KV), lambda i: (nbr(1)(i), 0)),
            pl.BlockSpec((ctx_len, W_KV), lambda i: (ctx_blk0 + bidx(i), 0)),
        ],
        out_specs=pl.BlockSpec((BLK, W_QB), lambda i: (i, 0)),
        compiler_params=pltpu.CompilerParams(
            dimension_semantics=("parallel",), vmem_limit_bytes=_vmem_limit(32 * 1024 * 1024)),
        name="attn",
    )(sink, qb_all, kv_all, kv_all, kv_all, kv_all)


def _merge_kernel(x_ref, ya_ref, yb_ref, gate_ref, woa_ref, wob_ref, wo_ref, g1_ref, n2_ref, sh2_ref, sc2_ref,
                  wrh_ref, wrl_ref, rb_ref, x1_ref, hp_ref, e12_ref, wk_ref):
    pa = _dot(ya_ref[...], woa_ref[...])
    pb = _dot(yb_ref[...], wob_ref[...])
    m = gate_ref[:, 0:D_MODEL].astype(f32) * pa + gate_ref[:, D_MODEL:2 * D_MODEL].astype(f32) * pb
    y = _dot(m.astype(bf16), wo_ref[...])
    x1 = x_ref[...] + g1_ref[...] * y
    x1_ref[...] = x1
    ms = jnp.mean(x1 * x1, axis=-1, keepdims=True)
    h2 = (x1 * lax.rsqrt(ms + EPS) * n2_ref[...]) * (1.0 + sc2_ref[...]) + sh2_ref[...]
    h_hi = h2.astype(bf16)
    hp_ref[...] = _pack_halves(h_hi.astype(f32))
    h_lo = (h2 - h_hi.astype(f32)).astype(bf16)

    logit = _dot_nt(wrh_ref[...], h_hi) + _dot_nt(wrl_ref[...], h_hi) + _dot_nt(wrh_ref[...], h_lo)
    score = _sigmoid(logit)
    sel = score + rb_ref[...]
    tm = sel.shape[1]

    def top2_in_group(g):
        r = [sel[g * EXP_PER_GROUP + j:g * EXP_PER_GROUP + j + 1, :] for j in range(EXP_PER_GROUP)]
        pair = None
        for a in range(EXP_PER_GROUP):
            for b in range(a + 1, EXP_PER_GROUP):
                s_ab = r[a] + r[b]
                pair = s_ab if pair is None else jnp.maximum(pair, s_ab)
        best = r[0]
        i1 = jnp.zeros((1, tm), jnp.int32)
        for j in range(1, EXP_PER_GROUP):
            better = r[j] > best
            best = jnp.where(better, r[j], best)
            i1 = jnp.where(better, j, i1)
        second = jnp.full((1, tm), -jnp.inf, f32)
        i2 = jnp.zeros((1, tm), jnp.int32)
        for j in range(EXP_PER_GROUP):
            cand = jnp.where(i1 == j, -jnp.inf, r[j])
            better = cand > second
            second = jnp.where(better, cand, second)
            i2 = jnp.where(better, j, i2)
        return pair, i1 + g * EXP_PER_GROUP, i2 + g * EXP_PER_GROUP

    gs, e1, e2 = top2_in_group(0)
    for g in range(1, N_GROUPS):
        gs_g, e1_g, e2_g = top2_in_group(g)
        better = gs_g > gs
        gs = jnp.where(better, gs_g, gs)
        e1 = jnp.where(better, e1_g, e1)
        e2 = jnp.where(better, e2_g, e2)

    eidx = lax.broadcasted_iota(jnp.int32, (N_EXPERTS, tm), 0)
    hit1 = eidx == e1
    hit2 = eidx == e2
    s1 = jnp.sum(jnp.where(hit1, score, 0.0), axis=0, keepdims=True)
    s2 = jnp.sum(jnp.where(hit2, score, 0.0), axis=0, keepdims=True)
    tot = s1 + s2
    e12_ref[...] = jnp.concatenate([e1, e2], axis=0)
    wk_pad = jnp.concatenate([s1 / tot, s2 / tot, jnp.zeros((LANES - 2, tm), f32)], axis=0)
    wk_ref[...] = wk_pad.T


def merge(x_all, ya_all, yb_all, gate_all, w_oa, w_ob, w_o, mod_l, norm2_g, wr_hi, wr_lo, router_bias,
          *, n_rows, n_lat_tiles_per_seq, n_lat_tiles, batch, tm):
    n_tiles = n_rows // tm

    def seg(i):
        return jnp.where(i < n_lat_tiles, i // n_lat_tiles_per_seq, batch)

    row = lambda w: pl.BlockSpec((tm, w), lambda i: (i, 0))
    full = lambda a, b: pl.BlockSpec((a, b), lambda i: (0, 0))
    modspec = lambda j: pl.BlockSpec((None, 1, D_MODEL), lambda i: (seg(i), 0, j))
    vmem = 2 * (2 * W_Z * D_MODEL * 2 + D_MODEL * D_MODEL * 2) + 2 * tm * (
        D_MODEL * 4 * 2 + 2 * W_Z * 2 + W_GATE * 2 + D_MODEL * 2 + LANES * 4) + 8 * tm * D_MODEL * 4
    return pl.pallas_call(
        _merge_kernel,
        out_shape=(
            jax.ShapeDtypeStruct((n_rows, D_MODEL), f32),
            jax.ShapeDtypeStruct((n_rows, D_MODEL // 2), jnp.uint32),
            jax.ShapeDtypeStruct((2, n_rows), jnp.int32),
            jax.ShapeDtypeStruct((n_rows, LANES), f32),
        ),
        grid=(n_tiles,),
        in_specs=[
            row(D_MODEL), row(W_Z), row(W_QB), row(W_GATE),
            full(W_Z, D_MODEL), full(W_QB, D_MODEL), full(D_MODEL, D_MODEL),
            modspec(2), full(1, D_MODEL), modspec(3), modspec(4),
            full(N_EXPERTS, D_MODEL), full(N_EXPERTS, D_MODEL), full(N_EXPERTS, 1),
        ],
        out_specs=(row(D_MODEL), row(D_MODEL // 2), pl.BlockSpec((2, tm), lambda i: (0, i)), row(LANES)),
        compiler_params=pltpu.CompilerParams(
            dimension_semantics=("parallel",), vmem_limit_bytes=_vmem_limit(vmem)),
        name="merge",
    )(x_all, ya_all, yb_all, gate_all, w_oa, w_ob, w_o, mod_l, norm2_g.reshape(1, D_MODEL), mod_l, mod_l,
      wr_hi, wr_lo, router_bias.reshape(N_EXPERTS, 1))


def _moe_kernel(ord_ref, tok_ref, st_ref, x1_ref, hp_ref, wk_ref, wg_ref, wu_ref, wd_ref, g2_ref, fg_ref, o_ref,
                slot_s, xg_s, yp_s, *, tm, final_norm):
    e = pl.program_id(1)
    half = D_MODEL // 2
    start = st_ref[0, e]
    count = st_ref[0, e + 1] - start

    @pl.when(e == 0)
    def _():
        xg_s[...] = jnp.zeros_like(xg_s)

    def chunk(c, carry):
        base = start + c * MOE_CHUNK
        ngroups = (jnp.minimum(count - c * MOE_CHUNK, MOE_CHUNK) + SUBLANES - 1) // SUBLANES

        def gather(g, carry2):
            for u in range(SUBLANES):
                r = g * SUBLANES + u
                xg_s[pl.ds(r, 1), :] = hp_ref[pl.ds(tok_ref[0, base + r], 1), :]
            return carry2

        lax.fori_loop(0, ngroups, gather, 0)
        x_lo, x_hi = _unpack_halves(xg_s[...])
        x_lo = x_lo.astype(bf16)
        x_hi = x_hi.astype(bf16)
        a = _dot(x_lo, wg_ref[0:half, :]) + _dot(x_hi, wg_ref[half:D_MODEL, :])
        b = _dot(x_lo, wu_ref[0:half, :]) + _dot(x_hi, wu_ref[half:D_MODEL, :])
        he = (_silu(a) * b).astype(bf16)
        y = _dot(he, wd_ref[...])
        yp_s[...] = _pack_halves(y)

        def scatter(g, carry2):
            for u in range(SUBLANES):
                r = g * SUBLANES + u
                slot_s[pl.ds(ord_ref[0, base + r], 1), :] = yp_s[pl.ds(r, 1), :]
            return carry2

        lax.fori_loop(0, ngroups, scatter, 0)
        return carry

    lax.fori_loop(0, (count + MOE_CHUNK - 1) // MOE_CHUNK, chunk, 0)

    @pl.when(e == N_EXPERTS - 1)
    def _():
        sub = 256
        lane = lax.broadcasted_iota(jnp.int32, (sub, LANES), 1)
        for r in range(tm // sub):
            rows = pl.ds(r * sub, sub)
            wk = wk_ref[rows, :]
            w1 = jnp.sum(jnp.where(lane == 0, wk, 0.0), axis=-1, keepdims=True)
            w2 = jnp.sum(jnp.where(lane == 1, wk, 0.0), axis=-1, keepdims=True)
            lo1, hi1 = _unpack_halves(slot_s[pl.ds(r * sub, sub), :])
            lo2, hi2 = _unpack_halves(slot_s[pl.ds(tm + r * sub, sub), :])
            acc = jnp.concatenate([w1 * lo1 + w2 * lo2, w1 * hi1 + w2 * hi2], axis=1)
            x2 = x1_ref[rows, :] + g2_ref[...] * acc
            if final_norm:
                ms = jnp.mean(x2 * x2, axis=-1, keepdims=True)
                x2 = x2 * lax.rsqrt(ms + EPS) * fg_ref[...]
            o_ref[rows, :] = x2


def moe(x1, hp, e12, wk, w_gate, w_up, w_down, mod_l, final_g, *, n_lat_tiles_per_seq, n_lat_tiles, batch, tm,
        final_norm):
    n_rows = x1.shape[0]
    n_tiles = n_rows // tm

    ids = e12.reshape(2, n_tiles, tm).transpose(1, 0, 2).reshape(n_tiles, 2 * tm)
    order = jnp.argsort(ids, axis=1, stable=True).astype(jnp.int32)
    tok = jnp.pad(jnp.where(order >= tm, order - tm, order), ((0, 0), (0, MOE_CHUNK)))
    order = jnp.pad(order, ((0, 0), (0, MOE_CHUNK)), constant_values=2 * tm)
    counts = jnp.sum((ids[:, :, None] == jnp.arange(N_EXPERTS, dtype=jnp.int32)[None, None, :]).astype(jnp.int32),
                     axis=1)
    starts = jnp.concatenate([jnp.zeros((n_tiles, 1), jnp.int32), jnp.cumsum(counts, axis=1)], axis=1)
    starts = jnp.pad(starts, ((0, 0), (0, 2 * N_EXPERTS - (N_EXPERTS + 1))))

    def seg(i):
        return jnp.where(i < n_lat_tiles, i // n_lat_tiles_per_seq, batch)

    row = lambda w: pl.BlockSpec((tm, w), lambda i, e: (i, 0))
    smem_row = lambda w: pl.BlockSpec((None, 1, w), lambda i, e: (i, 0, 0), memory_space=pltpu.SMEM)
    kern = functools.partial(_moe_kernel, tm=tm, final_norm=final_norm)
    vmem = (2 * 3 * D_MODEL * D_EXPERT * 2 + 2 * tm * (D_MODEL * 4 * 2 + D_MODEL * 2 + LANES * 4)
            + 2 * tm * D_MODEL * 2 + 16 * 1024 * 1024)
    return pl.pallas_call(
        kern,
        out_shape=jax.ShapeDtypeStruct((n_rows, D_MODEL), f32),
        grid=(n_tiles, N_EXPERTS),
        in_specs=[
            smem_row(2 * tm + MOE_CHUNK), smem_row(2 * tm + MOE_CHUNK), smem_row(2 * N_EXPERTS),
            row(D_MODEL), row(D_MODEL // 2), row(LANES),
            pl.BlockSpec((None, D_MODEL, D_EXPERT), lambda i, e: (e, 0, 0)),
            pl.BlockSpec((None, D_MODEL, D_EXPERT), lambda i, e: (e, 0, 0)),
            pl.BlockSpec((None, D_EXPERT, D_MODEL), lambda i, e: (e, 0, 0)),
            pl.BlockSpec((None, 1, D_MODEL), lambda i, e: (seg(i), 0, 5)),
            pl.BlockSpec((1, D_MODEL), lambda i, e: (0, 0)),
        ],
        out_specs=row(D_MODEL),
        scratch_shapes=[
            pltpu.VMEM((2 * tm + SUBLANES, D_MODEL // 2), jnp.uint32),
            pltpu.VMEM((MOE_CHUNK, D_MODEL // 2), jnp.uint32),
            pltpu.VMEM((MOE_CHUNK, D_MODEL // 2), jnp.uint32),
        ],
        compiler_params=pltpu.CompilerParams(
            dimension_semantics=("parallel", "arbitrary"), vmem_limit_bytes=_vmem_limit(vmem)),
        name="moe",
    )(order.reshape(n_tiles, 1, 2 * tm + MOE_CHUNK), tok.reshape(n_tiles, 1, 2 * tm + MOE_CHUNK),
      starts.reshape(n_tiles, 1, 2 * N_EXPERTS), x1, hp, wk,
      w_gate, w_up, w_down, mod_l, final_g.reshape(1, D_MODEL))


def _prep_w_in(w_in_l):
    qkv_a = 3 * H_A * DK_A
    o_z = qkv_a
    o_b = o_z + H_A * DV_A
    o_a = o_b + 2 * H_A
    o_q = o_a + 2 * H_A
    o_k = o_q + HQ_B * DH_B
    o_v = o_k + HKV_B * DH_B
    o_ga = o_v + HKV_B * DH_B
    w_bg = jnp.zeros((D_MODEL, W_BG), f32).at[:, :4 * H_A].set(w_in_l[:, o_b:o_q])
    wq = w_in_l[:, o_q:o_k].reshape(D_MODEL, HQ_B, DH_B)[:, np.array(HEAD_PERM), :].reshape(D_MODEL, W_QB)
    wq = wq * (DH_B ** -0.5)
    cat = jnp.concatenate(
        [w_in_l[:, :qkv_a], w_in_l[:, o_z:o_b], w_bg, wq, w_in_l[:, o_k:o_v], w_in_l[:, o_v:o_ga],
         w_in_l[:, o_ga:]], axis=1)
    return cat.astype(bf16)


def _rope_tables(seq, tm):
    t = np.arange(seq)
    rows = (t // GRID_W).astype(np.float32)
    cols = (t % GRID_W).astype(np.float32)
    nf = DH_B // 4
    inv = jnp.asarray(ROPE_THETA, f32) ** (-jnp.arange(nf, dtype=f32) / nf)
    ang_r = jnp.asarray(rows)[:, None] * inv[None, :]
    ang_c = jnp.asarray(cols)[:, None] * inv[None, :]
    cos64 = jnp.concatenate([jnp.cos(ang_r), jnp.cos(ang_r), jnp.cos(ang_c), jnp.cos(ang_c)], axis=1)
    zero = jnp.zeros_like(ang_r)
    sa64 = jnp.concatenate([-jnp.sin(ang_r), zero, -jnp.sin(ang_c), zero], axis=1)
    sb64 = jnp.concatenate([zero, jnp.sin(ang_r), zero, jnp.sin(ang_c)], axis=1)

    def fin(tab, fill):
        tab = jnp.concatenate([tab, tab], axis=1)
        return jnp.concatenate([tab, jnp.full((tm, LANES), fill, f32)], axis=0)

    return fin(cos64, 1.0), fin(sa64, 0.0), fin(sb64, 0.0)


TM_TOKEN = 256
TM_MOE = 1024


def kernel(x, c, ctx, c_ctx, w_mod, b_mod, norm1_g, norm2_g, w_in, conv_w, a_log, dt_bias, gdn_norm_g, sink,
           w_oa, w_ob, w_o, w_router, router_bias, w_gate, w_up, w_down, final_g):
    batch, seq, d = x.shape
    ctx_len = ctx.shape[1]
    depth = w_mod.shape[0]
    assert d == D_MODEL and seq % SUPER == 0 and ctx_len % SUPER == 0 and seq % TM_MOE == 0
    t_lat = batch * seq
    t_ctx = batch * ctx_len
    assert t_ctx % TM_MOE == 0

    x_all = jnp.concatenate([x.reshape(t_lat, d), ctx.reshape(t_ctx, d)], axis=0)
    t_all = t_lat + t_ctx

    n_cond = ((batch + 1 + SUBLANES - 1) // SUBLANES) * SUBLANES
    cond = jnp.zeros((n_cond, d), f32).at[:batch].set(c).at[batch].set(c_ctx)
    mod = adaln_all(cond, w_mod, b_mod).reshape(depth, n_cond, 1, 6 * d)

    rope_tabs = _rope_tables(seq, TM_TOKEN)
    wr_hi = w_router.T.astype(bf16)
    wr_lo = (w_router.T - wr_hi.astype(f32)).astype(bf16)
    perm_rows = np.concatenate([np.arange(h * DH_B, (h + 1) * DH_B) for h in HEAD_PERM])
    zero_state = jnp.zeros((batch, H_A, DK_A, DV_A), f32)

    tok = dict(n_lat_tiles_per_seq=seq // TM_TOKEN, n_lat_tiles=t_lat // TM_TOKEN, batch=batch, tm=TM_TOKEN)
    tok_moe = dict(n_lat_tiles_per_seq=seq // TM_MOE, n_lat_tiles=t_lat // TM_MOE, batch=batch, tm=TM_MOE)

    for l in range(depth):
        need_ctx = l < depth - 1
        w_cat = _prep_w_in(w_in[l])
        qkv_all, z_all, bg_all, qb_all, kv_all, gate_all = inproj(
            x_all, mod[l], norm1_g[l], w_cat, rope_tabs, a_log[l], dt_bias[l], **tok)

        ya_all = jnp.zeros((t_all, W_Z), bf16)
        ya_all, s_f, s_b = gdn(qkv_all, bg_all, z_all, ya_all, conv_w[l], gdn_norm_g[l], zero_state, zero_state,
                               batch=batch, seq=ctx_len, row_block0=t_lat // ctx_len)
        ya_all, _, _ = gdn(qkv_all, bg_all, z_all, ya_all, conv_w[l], gdn_norm_g[l], s_f, s_b,
                           batch=batch, seq=seq, row_block0=0)

        yb_all = attention(qb_all, kv_all, sink[l], batch=batch, seq=seq, ctx_len=ctx_len,
                           with_ctx_queries=need_ctx)

        n_rows = t_all if need_ctx else t_lat
        x1, hp, e12, wk = merge(
            x_all, ya_all, yb_all, gate_all, w_oa[l].astype(bf16), w_ob[l][perm_rows].astype(bf16),
            w_o[l].astype(bf16), mod[l], norm2_g[l], wr_hi, wr_lo, router_bias, n_rows=n_rows, **tok)
        x_all = moe(x1, hp, e12, wk, w_gate[l].astype(bf16), w_up[l].astype(bf16), w_down[l].astype(bf16),
                    mod[l], final_g, final_norm=not need_ctx, **tok_moe)

    return x_all[:t_lat].reshape(batch, seq, d)
```

```python
import functools

import jax
import jax.numpy as jnp
import numpy as np
from jax import lax
from jax.experimental import pallas as pl
from jax.experimental.pallas import tpu as pltpu

f32 = jnp.float32
bf16 = jnp.bfloat16

D_MODEL = 1024
GRID_W = 64
EPS = 1e-6
H_A = 4
DK_A = 128
DV_A = 128
CONV_K = 5
CHUNK = 64
HQ_B = 8
HKV_B = 2
DH_B = 64
WINDOW = 128
BLK = 128
ROPE_THETA = 10000.0
N_EXPERTS = 16
N_GROUPS = 4
EXP_PER_GROUP = 4
D_EXPERT = 512

LANES = 128
SUBLANES = 8
BF16_ROWS = 16
VMEM_BYTES_V7X = 64 * 1024 * 1024

W_QKV = 3 * H_A * DK_A
W_Z = H_A * DV_A
W_BG = LANES
W_QB = HQ_B * DH_B
W_KV = 2 * HKV_B * DH_B
W_GATE = 2 * D_MODEL
OFF_Z = W_QKV
OFF_BG = OFF_Z + W_Z
OFF_QB = OFF_BG + W_BG
OFF_KV = OFF_QB + W_QB
OFF_GATE = OFF_KV + W_KV
W_TOTAL = OFF_GATE + W_GATE
W_ROPE = W_QB + HKV_B * DH_B

NEG_BIG = -1e30
HEAD_PERM = (0, 4, 1, 5, 2, 6, 3, 7)

SUPER = 4 * CHUNK
HALF = CHUNK // 2
MOE_CHUNK = 192


def _vmem_limit(nbytes):
    return int(min(max(nbytes, 16 * 1024 * 1024), VMEM_BYTES_V7X - 8 * 1024 * 1024))


def _split3(x):
    hi = x.astype(bf16)
    r1 = x - hi.astype(f32)
    mid = r1.astype(bf16)
    lo = (r1 - mid.astype(f32)).astype(bf16)
    return hi, mid, lo


def _dot(a, b):
    return jnp.dot(a, b, preferred_element_type=f32)


def _dot_nt(a, b):
    return lax.dot_general(a, b, (((1,), (1,)), ((), ())), preferred_element_type=f32)


def _dot_exact_lhs_mask(mask_bf, parts):
    return _dot(mask_bf, parts[0]) + _dot(mask_bf, parts[1]) + _dot(mask_bf, parts[2])


def _pack_halves(x):
    w = x.shape[1] // 2
    lo = lax.bitcast_convert_type(x[:, 0:w].astype(bf16).astype(f32), jnp.uint32)
    hi = lax.bitcast_convert_type(x[:, w:2 * w].astype(bf16).astype(f32), jnp.uint32)
    return lax.shift_right_logical(lo, jnp.uint32(16)) | (hi & jnp.uint32(0xFFFF0000))


def _unpack_halves(p):
    lo = lax.bitcast_convert_type(lax.shift_left(p, jnp.uint32(16)), f32)
    hi = lax.bitcast_convert_type(p & jnp.uint32(0xFFFF0000), f32)
    return lo, hi


def _sigmoid(x):
    return 1.0 / (1.0 + jnp.exp(-x))


def _silu(x):
    return x * _sigmoid(x)


def _softplus(x):
    return jnp.maximum(x, 0.0) + jnp.log1p(jnp.exp(-jnp.abs(x)))


def _adaln_kernel(c_ref, w_ref, b_ref, o_ref):
    c = c_ref[...]
    s = _silu(c).astype(bf16)
    o_ref[...] = _dot(s, w_ref[...].astype(bf16)) + b_ref[...]


def adaln_all(cond, w_mod, b_mod):
    depth = w_mod.shape[0]
    r = cond.shape[0]
    nblk = 6
    return pl.pallas_call(
        _adaln_kernel,
        out_shape=jax.ShapeDtypeStruct((depth, r, 6 * D_MODEL), f32),
        grid=(depth, nblk),
        in_specs=[
            pl.BlockSpec((r, D_MODEL), lambda l, j: (0, 0)),
            pl.BlockSpec((None, D_MODEL, D_MODEL), lambda l, j: (l, 0, j)),
            pl.BlockSpec((None, 1, D_MODEL), lambda l, j: (l, 0, j)),
        ],
        out_specs=pl.BlockSpec((None, r, D_MODEL), lambda l, j: (l, 0, j)),
        compiler_params=pltpu.CompilerParams(
            dimension_semantics=("parallel", "parallel"),
            vmem_limit_bytes=_vmem_limit(24 * 1024 * 1024)),
        name="adaln",
    )(cond, w_mod, b_mod.reshape(depth, 1, 6 * D_MODEL))


def _inproj_kernel(x_ref, g_ref, sh_ref, sc_ref, cos_ref, sa_ref, sb_ref, alog_ref, dtb_ref, w_ref,
                   qkv_ref, z_ref, bg_ref, qb_ref, kv_ref, gate_ref):
    x = x_ref[...]
    ms = jnp.mean(x * x, axis=-1, keepdims=True)
    y = x * lax.rsqrt(ms + EPS) * g_ref[...]
    h = (y * (1.0 + sc_ref[...]) + sh_ref[...]).astype(bf16)

    qkv_ref[...] = _dot(h, w_ref[:, 0:W_QKV]).astype(bf16)
    z_ref[...] = _dot(h, w_ref[:, OFF_Z:OFF_Z + W_Z]).astype(bf16)
    raw = _dot(h, w_ref[:, OFF_BG:OFF_BG + W_BG])
    lane = lax.broadcasted_iota(jnp.int32, raw.shape, 1)
    bg_ref[...] = jnp.where(lane < 2 * H_A, _sigmoid(raw),
                            -jnp.exp(alog_ref[...]) * _softplus(raw + dtb_ref[...]))

    pr = _dot(h, w_ref[:, OFF_QB:OFF_QB + W_ROPE])
    reps = W_ROPE // LANES
    cos = jnp.concatenate([cos_ref[...]] * reps, axis=1)
    sa = jnp.concatenate([sa_ref[...]] * reps, axis=1)
    sb = jnp.concatenate([sb_ref[...]] * reps, axis=1)
    half = DH_B // 4
    rot = pr * cos + pltpu.roll(pr, W_ROPE - half, 1) * sa + pltpu.roll(pr, half, 1) * sb
    qb_ref[...] = rot[:, 0:W_QB].astype(bf16)
    kv_ref[:, 0:LANES] = rot[:, W_QB:W_ROPE].astype(bf16)
    kv_ref[:, LANES:2 * LANES] = _dot(h, w_ref[:, OFF_KV + LANES:OFF_KV + 2 * LANES]).astype(bf16)

    gate_ref[...] = _sigmoid(_dot(h, w_ref[:, OFF_GATE:OFF_GATE + W_GATE])).astype(bf16)


def inproj(x_all, mod_l, norm_g, w_cat, rope_tabs, a_log_l, dt_bias_l, *, n_lat_tiles_per_seq, n_lat_tiles, batch,
           tm):
    t_all = x_all.shape[0]
    n_tiles = t_all // tm
    cos_t, sa_t, sb_t = rope_tabs
    pad_lanes = lambda v: jnp.zeros((1, LANES), f32).at[0, 2 * H_A:4 * H_A].set(v.reshape(-1))
    alog_v = pad_lanes(a_log_l)
    dtb_v = pad_lanes(dt_bias_l)

    def seg(i):
        return jnp.where(i < n_lat_tiles, i // n_lat_tiles_per_seq, batch)

    def pos(i):
        return jnp.where(i < n_lat_tiles, i % n_lat_tiles_per_seq, n_lat_tiles_per_seq)

    row = lambda w: pl.BlockSpec((tm, w), lambda i: (i, 0))
    tab = pl.BlockSpec((tm, LANES), lambda i: (pos(i), 0))
    out_shapes = (
        jax.ShapeDtypeStruct((t_all, W_QKV), bf16),
        jax.ShapeDtypeStruct((t_all, W_Z), bf16),
        jax.ShapeDtypeStruct((t_all, W_BG), f32),
        jax.ShapeDtypeStruct((t_all, W_QB), bf16),
        jax.ShapeDtypeStruct((t_all, W_KV), bf16),
        jax.ShapeDtypeStruct((t_all, W_GATE), bf16),
    )
    vmem = 2 * (W_TOTAL * D_MODEL * 2) + 2 * tm * (D_MODEL * 4 + W_TOTAL * 2 + 3 * LANES * 4) + tm * W_TOTAL * 4
    return pl.pallas_call(
        _inproj_kernel,
        out_shape=out_shapes,
        grid=(n_tiles,),
        in_specs=[
            row(D_MODEL),
            pl.BlockSpec((1, D_MODEL), lambda i: (0, 0)),
            pl.BlockSpec((None, 1, D_MODEL), lambda i: (seg(i), 0, 0)),
            pl.BlockSpec((None, 1, D_MODEL), lambda i: (seg(i), 0, 1)),
            tab, tab, tab,
            pl.BlockSpec((1, LANES), lambda i: (0, 0)),
            pl.BlockSpec((1, LANES), lambda i: (0, 0)),
            pl.BlockSpec((D_MODEL, W_TOTAL), lambda i: (0, 0)),
        ],
        out_specs=(row(W_QKV), row(W_Z), row(W_BG), row(W_QB), row(W_KV), row(W_GATE)),
        compiler_params=pltpu.CompilerParams(
            dimension_semantics=("parallel",), vmem_limit_bytes=_vmem_limit(vmem)),
        name="inproj",
    )(x_all, norm_g.reshape(1, D_MODEL), mod_l, mod_l, cos_t, sa_t, sb_t, alog_v, dtb_v, w_cat)


def _gdn_kernel(q_ref, k_ref, v_ref, bg_ref, z_ref, cwq_ref, cwk_ref, cwv_ref, gn_ref, s0f_ref, s0b_ref,
                yin_ref, y_ref, sf_ref, sb_ref,
                u_s, w_s, qg_s, ak_s, gl_s, o_s, st_s, a2_s, x_s, *, seq, nsuper):
    del yin_ref
    hd = pl.program_id(1)
    nch = seq // CHUNK
    cps = SUPER // CHUNK

    ii = lax.broadcasted_iota(jnp.int32, (SUPER, SUPER), 0)
    jj = lax.broadcasted_iota(jnp.int32, (SUPER, SUPER), 1)
    same = (ii // CHUNK) == (jj // CHUNK)
    tri = (same & (jj <= ii), same & (jj >= ii))
    tri_s = (same & (jj < ii), same & (jj > ii))
    cum2_bf = jnp.concatenate([jnp.where(m, 1.0, 0.0) for m in tri], axis=0).astype(bf16)
    lane = lax.broadcasted_iota(jnp.int32, (SUPER, LANES), 1)

    def conv_silu(ref, cw_ref, base, st):
        main = ref[pl.ds(base, SUPER), :].astype(f32)
        p0 = pl.multiple_of(jnp.maximum(base - BF16_ROWS, 0), BF16_ROWS)
        n0 = pl.multiple_of(jnp.minimum(base + SUPER, seq - BF16_ROWS), BF16_ROWS)
        prev = ref[pl.ds(p0, BF16_ROWS), :].astype(f32) * jnp.where(st > 0, 1.0, 0.0)
        nxt = ref[pl.ds(n0, BF16_ROWS), :].astype(f32) * jnp.where(st < nsuper - 1, 1.0, 0.0)
        win = jnp.concatenate([prev, main, nxt], axis=0)
        acc = jnp.zeros((SUPER, LANES), f32)
        for i in range(CONV_K):
            lo_row = BF16_ROWS - CONV_K // 2 + i
            acc = acc + win[lo_row:lo_row + SUPER, :] * cw_ref[i:i + 1, :]
        return _silu(acc)

    def column(x, c):
        col = jnp.sum(jnp.where(lane == c, x, 0.0), axis=-1, keepdims=True)
        return jnp.broadcast_to(col, (SUPER, LANES))

    def front(st):
        base = pl.multiple_of(st * SUPER, SUPER)
        q = conv_silu(q_ref, cwq_ref, base, st)
        yield
        k = conv_silu(k_ref, cwk_ref, base, st)
        yield
        v = conv_silu(v_ref, cwv_ref, base, st)
        yield
        q = q * lax.rsqrt(jnp.sum(q * q, axis=-1, keepdims=True) + EPS) * (DK_A ** -0.5)
        k = k * lax.rsqrt(jnp.sum(k * k, axis=-1, keepdims=True) + EPS)
        q_bf = q.astype(bf16)
        k_bf = k.astype(bf16)
        qkk = _dot_nt(jnp.concatenate([q_bf, k_bf], axis=0), k_bf)
        qk_raw = qkk[0:SUPER]
        kk_raw = qkk[SUPER:2 * SUPER]
        bg = bg_ref[pl.ds(base, SUPER), :]
        rows = pl.ds(base, SUPER)
        yield

        g_both = [column(bg, 2 * H_A + d * H_A + hd) for d in range(2)]
        parts = [jnp.concatenate(p, axis=1) for p in zip(_split3(g_both[0]), _split3(g_both[1]))]
        csum_both = _dot_exact_lhs_mask(cum2_bf, parts)
        yield

        for d in range(2):
            beta = column(bg, d * H_A + hd)
            mask, mask_s = tri[d], tri_s[d]
            csum = csum_both[d * SUPER:(d + 1) * SUPER, d * LANES:(d + 1) * LANES]
            ct = jnp.concatenate([csum[0:LANES].T, csum[LANES:2 * LANES].T], axis=1)
            ct = jnp.concatenate([ct, ct], axis=0)
            c2 = jnp.concatenate([csum, csum], axis=1)
            decay = jnp.where(mask, jnp.exp(jnp.where(mask, c2 - ct, 0.0)), 0.0)
            yield
            beta2 = jnp.concatenate([beta, beta], axis=1)
            a_mat = jnp.where(mask_s, beta2 * kk_raw * decay, 0.0)
            if d == 1:
                a_mat = a_mat.T
            for c in range(cps):
                cs = slice(c * CHUNK, (c + 1) * CHUNK)
                blk = a_mat[cs, cs]
                prob = d * nch + st * cps + c
                a2_s[pl.ds(pl.multiple_of(prob * HALF, HALF), HALF), :] = jnp.concatenate(
                    [blk[0:HALF], blk[HALF:CHUNK]], axis=1)
            attn = jnp.where(mask, qk_raw * decay, 0.0)
            yield

            last = (CHUNK - 1, 0)[d]
            tot = jnp.concatenate(
                [jnp.broadcast_to(csum[c * CHUNK + last:c * CHUNK + last + 1, :], (CHUNK, LANES))
                 for c in range(cps)], axis=0)
            egc = jnp.exp(csum)
            u_s[d, rows, :] = (v * beta).astype(bf16)
            w_s[d, rows, :] = (k * beta * egc).astype(bf16)
            qg_s[d, rows, :] = (q * egc).astype(bf16)
            yield
            kg_t = (k * jnp.exp(tot - csum)).T
            glast = jnp.exp(tot)
            for c in range(cps):
                cs = slice(c * CHUNK, (c + 1) * CHUNK)
                ak_s[d, st * cps + c] = jnp.concatenate([attn[cs, cs], kg_t[:, cs]], axis=0).astype(bf16)
                gl_s[d, pl.ds(st * cps + c, 1), :] = glast[c * CHUNK:c * CHUNK + 1, :]
            yield

    def back(st):
        rows = pl.ds(pl.multiple_of(st * SUPER, SUPER), SUPER)
        lane2 = lax.broadcasted_iota(jnp.int32, (CHUNK, SUPER), 1)
        for d in range(2):
            blocks = []
            for c in range(cps):
                prob = d * nch + st * cps + c
                half = a2_s[pl.ds(pl.multiple_of(prob * HALF, HALF), HALF), :]
                blk = jnp.concatenate([half[:, 0:CHUNK], half[:, CHUNK:2 * CHUNK]], axis=0)
                wide = jnp.concatenate([blk] * cps, axis=1)
                blocks.append(jnp.where(lane2 // CHUNK == c, wide, 0.0))
            t_m = jnp.concatenate(blocks, axis=0)
            if d == 1:
                t_m = t_m.T
            yield
            rhs = jnp.concatenate([u_s[d, rows, :], w_s[d, rows, :]], axis=1)
            sol = rhs.astype(f32) + _dot(t_m.astype(bf16), rhs)
            u_s[d, rows, :] = sol[:, 0:DV_A].astype(bf16)
            w_s[d, rows, :] = sol[:, DV_A:2 * DV_A].astype(bf16)
            yield

    def interleave(*gens):
        active = list(gens)
        while active:
            for g in list(active):
                if next(g, StopIteration) is StopIteration:
                    active.remove(g)

    nprob = 2 * nch
    if nprob < LANES:
        a2_s[...] = jnp.zeros_like(a2_s)

    def front_loop(st, carry):
        interleave(front(st))
        return carry

    lax.fori_loop(0, nsuper, front_loop, 0)

    for r in range(HALF):
        xt = a2_s[pl.ds(r, LANES, stride=HALF), :].T
        x_s[r] = xt[0:CHUNK]
        x_s[r + HALF] = xt[CHUNK:2 * CHUNK]

    nblk = CHUNK // SUBLANES
    for i in range(1, CHUNK):
        acc = [-x_s[i, jb * SUBLANES:(jb + 1) * SUBLANES, :] for jb in range(nblk)]
        for k in range(1, i):
            a_ik = x_s[i, k:k + 1, :]
            for jb in range((k - 1) // SUBLANES + 1):
                acc[jb] = acc[jb] - a_ik * x_s[k, jb * SUBLANES:(jb + 1) * SUBLANES, :]
        for jb in range(nblk):
            x_s[i, jb * SUBLANES:(jb + 1) * SUBLANES, :] = acc[jb]

    for r in range(HALF):
        yt = jnp.concatenate([x_s[r], x_s[r + HALF]], axis=0).T
        a2_s[pl.ds(r, LANES, stride=HALF), :] = yt

    def back_loop(st, carry):
        interleave(back(st))
        return carry

    lax.fori_loop(0, nsuper, back_loop, 0)

    st_s[0] = s0f_ref[...]
    st_s[1] = s0b_ref[...]

    def scan_step(n, accumulate):
        for d in range(2):
            c = n if d == 0 else nch - 1 - n
            rows = pl.ds(pl.multiple_of(c * CHUNK, CHUNK), CHUNK)
            s_old = st_s[d]
            ws = _dot(jnp.concatenate([w_s[d, rows, :], qg_s[d, rows, :]], axis=0), s_old.astype(bf16))
            v_new = (u_s[d, rows, :].astype(f32) - ws[0:CHUNK]).astype(bf16)
            ov = _dot(ak_s[d, c], v_new)
            o = ws[CHUNK:2 * CHUNK] + ov[0:CHUNK]
            st_s[d] = s_old * gl_s[d, pl.ds(c, 1), :] + ov[CHUNK:CHUNK + DK_A]
            if accumulate:
                o_s[rows, :] = o_s[rows, :] + o
            else:
                o_s[rows, :] = o

    def scan_a(n, carry):
        scan_step(n, False)
        return carry

    def scan_b(n, carry):
        scan_step(n, True)
        return carry

    lax.fori_loop(0, nch // 2, scan_a, 0)
    lax.fori_loop(nch // 2, nch, scan_b, 0)

    def finish(st, carry):
        rows = pl.ds(pl.multiple_of(st * SUPER, SUPER), SUPER)
        o = o_s[rows, :]
        yn = o * lax.rsqrt(jnp.mean(o * o, axis=-1, keepdims=True) + EPS) * gn_ref[...]
        y_ref[rows, :] = (yn * _silu(z_ref[rows, :].astype(f32))).astype(bf16)
        return carry

    lax.fori_loop(0, nsuper, finish, 0)
    sf_ref[...] = st_s[0]
    sb_ref[...] = st_s[1]


def gdn(qkv_all, bg_all, z_all, ya_all, conv_w, gdn_g, s0f, s0b, *, batch, seq, row_block0):
    nsuper = seq // SUPER
    nch = seq // CHUNK
    nch_pad = max(nch, SUBLANES)
    cw = jnp.zeros((3 * H_A, SUBLANES, LANES), f32).at[:, :CONV_K, :].set(
        conv_w.reshape(CONV_K, 3 * H_A, LANES).transpose(1, 0, 2))

    col = lambda off: pl.BlockSpec((seq, LANES), lambda b, h: (row_block0 + b, off + h))
    cwspec = lambda off: pl.BlockSpec((None, SUBLANES, LANES), lambda b, h: (off + h, 0, 0))
    st_spec = pl.BlockSpec((None, None, DK_A, DV_A), lambda b, h: (b, h, 0, 0))
    kern = functools.partial(_gdn_kernel, seq=seq, nsuper=nsuper)
    vmem = (2 * seq * LANES * (3 * 2 + 2 + 4 + 2) + seq * LANES * (6 * 2 + 6 * 2 + 4)
            + 4 * 4 * DK_A * DV_A * 4 + 20 * 1024 * 1024)
    return pl.pallas_call(
        kern,
        out_shape=(
            jax.ShapeDtypeStruct(ya_all.shape, bf16),
            jax.ShapeDtypeStruct((batch, H_A, DK_A, DV_A), f32),
            jax.ShapeDtypeStruct((batch, H_A, DK_A, DV_A), f32),
        ),
        grid=(batch, H_A),
        in_specs=[
            col(0), col(H_A), col(2 * H_A),
            pl.BlockSpec((seq, LANES), lambda b, h: (row_block0 + b, 0)),
            col(0),
            cwspec(0), cwspec(H_A), cwspec(2 * H_A),
            pl.BlockSpec((1, DV_A), lambda b, h: (0, 0)),
            st_spec, st_spec,
            pl.BlockSpec(memory_space=pl.ANY),
        ],
        out_specs=(col(0), st_spec, st_spec),
        scratch_shapes=[
            pltpu.VMEM((2, seq, LANES), bf16),
            pltpu.VMEM((2, seq, LANES), bf16),
            pltpu.VMEM((2, seq, LANES), bf16),
            pltpu.VMEM((2, nch, CHUNK + DK_A, CHUNK), bf16),
            pltpu.VMEM((2, nch_pad, LANES), f32),
            pltpu.VMEM((seq, LANES), f32),
            pltpu.VMEM((2, DK_A, DV_A), f32),
            pltpu.VMEM((LANES * HALF, LANES), f32),
            pltpu.VMEM((CHUNK, CHUNK, LANES), f32),
        ],
        input_output_aliases={11: 0},
        compiler_params=pltpu.CompilerParams(
            dimension_semantics=("parallel", "parallel"), vmem_limit_bytes=_vmem_limit(vmem)),
        name="gdn",
    )(qkv_all, qkv_all, qkv_all, bg_all, z_all, cw, cw, cw, gdn_g.reshape(1, DV_A), s0f, s0b, ya_all)


def _attn_kernel(sink_ref, q_ref, kp_ref, kc_ref, kn_ref, kx_ref, o_ref, *, n_lat_blocks, blocks_per_seq):
    i = pl.program_id(0)
    is_lat = i < n_lat_blocks
    n = i % blocks_per_seq
    kv = jnp.concatenate([kp_ref[...], kc_ref[...], kn_ref[...], kx_ref[...]], axis=0)
    k2 = kv[:, 0:LANES]
    v2 = kv[:, LANES:2 * LANES]
    nctx = kx_ref.shape[0]

    row = lax.broadcasted_iota(jnp.int32, (BLK, BLK), 0)
    col = lax.broadcasted_iota(jnp.int32, (BLK, BLK), 1)
    ok_prev = (col >= row) & is_lat & (n > 0)
    ok_cent = jnp.broadcast_to(is_lat, (BLK, BLK))
    ok_next = (col <= row) & is_lat & (n < blocks_per_seq - 1)
    zeros = jnp.zeros((BLK, BLK), f32)
    bias = jnp.concatenate(
        [jnp.where(ok_prev, zeros, NEG_BIG), jnp.where(ok_cent, zeros, NEG_BIG),
         jnp.where(ok_next, zeros, NEG_BIG), jnp.zeros((BLK, nctx), f32)], axis=1)
    bias2 = jnp.concatenate([bias, bias], axis=0)

    lane = lax.broadcasted_iota(jnp.int32, (BLK, LANES), 1)
    lo = lane < DH_B
    lo_bf = jnp.where(lo, 1.0, 0.0).astype(bf16)
    hi_bf = jnp.where(lo, 0.0, 1.0).astype(bf16)
    rows2 = lax.broadcasted_iota(jnp.int32, (2 * BLK, 1), 0)
    for p in range(HQ_B // 2):
        q2 = q_ref[:, p * LANES:(p + 1) * LANES]
        qs = jnp.concatenate([q2 * lo_bf, q2 * hi_bf], axis=0)
        s = _dot_nt(qs, k2) + bias2
        sink = jnp.where(rows2 < BLK, sink_ref[p], sink_ref[HQ_B // 2 + p])
        m = jnp.maximum(jnp.max(s, axis=-1, keepdims=True), sink)
        e = jnp.exp(s - m)
        den = jnp.sum(e, axis=-1, keepdims=True) + jnp.exp(sink - m)
        o = _dot(e.astype(bf16), v2) * (1.0 / den)
        o_ref[:, p * LANES:(p + 1) * LANES] = jnp.where(lo, o[0:BLK], o[BLK:2 * BLK]).astype(bf16)


def attention(qb_all, kv_all, sink, *, batch, seq, ctx_len, with_ctx_queries):
    t_lat = batch * seq
    bps = seq // BLK
    n_lat = t_lat // BLK
    cps = ctx_len // BLK
    n_blocks = n_lat + (batch * cps if with_ctx_queries else 0)
    ctx_blk0 = t_lat // ctx_len

    def bidx(i):
        return jnp.where(i < n_lat, i // bps, (i - n_lat) // cps)

    def nbr(delta):
        def f(i):
            n = i % bps
            j = jnp.clip(n + delta, 0, bps - 1)
            return jnp.where(i < n_lat, (i // bps) * bps + j, 0)
        return f

    kern = functools.partial(_attn_kernel, n_lat_blocks=n_lat, blocks_per_seq=bps)
    return pl.pallas_call(
        kern,
        out_shape=jax.ShapeDtypeStruct((n_blocks * BLK, W_QB), bf16),
        grid=(n_blocks,),
        in_specs=[
            pl.BlockSpec(memory_space=pltpu.SMEM),
            pl.BlockSpec((BLK, W_QB), lambda i: (i, 0)),
            pl.BlockSpec((BLK, W_KV), lambda i: (nbr(-1)(i), 0)),
            pl.BlockSpec((BLK, W_KV), lambda i: (nbr(0)(i), 0)),
            pl.BlockSpec((BLK, W_KV), lambda i: (nbr(1)(i), 0)),
            pl.BlockSpec((ctx_len, W_KV), lambda i: (ctx_blk0 + bidx(i), 0)),
        ],
        out_specs=pl.BlockSpec((BLK, W_QB), lambda i: (i, 0)),
        compiler_params=pltpu.CompilerParams(
            dimension_semantics=("parallel",), vmem_limit_bytes=_vmem_limit(32 * 1024 * 1024)),
        name="attn",
    )(sink, qb_all, kv_all, kv_all, kv_all, kv_all)


def _merge_kernel(x_ref, ya_ref, yb_ref, gate_ref, woa_ref, wob_ref, wo_ref, g1_ref, n2_ref, sh2_ref, sc2_ref,
                  wrh_ref, wrl_ref, rb_ref, x1_ref, hp_ref, e12_ref, wk_ref):
    pa = _dot(ya_ref[...], woa_ref[...])
    pb = _dot(yb_ref[...], wob_ref[...])
    m = gate_ref[:, 0:D_MODEL].astype(f32) * pa + gate_ref[:, D_MODEL:2 * D_MODEL].astype(f32) * pb
    y = _dot(m.astype(bf16), wo_ref[...])
    x1 = x_ref[...] + g1_ref[...] * y
    x1_ref[...] = x1
    ms = jnp.mean(x1 * x1, axis=-1, keepdims=True)
    h2 = (x1 * lax.rsqrt(ms + EPS) * n2_ref[...]) * (1.0 + sc2_ref[...]) + sh2_ref[...]
    h_hi = h2.astype(bf16)
    hp_ref[...] = _pack_halves(h_hi.astype(f32))
    h_lo = (h2 - h_hi.astype(f32)).astype(bf16)

    logit = _dot_nt(wrh_ref[...], h_hi) + _dot_nt(wrl_ref[...], h_hi) + _dot_nt(wrh_ref[...], h_lo)
    score = _sigmoid(logit)
    sel = score + rb_ref[...]
    tm = sel.shape[1]

    def top2_in_group(g):
        r = [sel[g * EXP_PER_GROUP + j:g * EXP_PER_GROUP + j + 1, :] for j in range(EXP_PER_GROUP)]
        pair = None
        for a in range(EXP_PER_GROUP):
            for b in range(a + 1, EXP_PER_GROUP):
                s_ab = r[a] + r[b]
                pair = s_ab if pair is None else jnp.maximum(pair, s_ab)
        best = r[0]
        i1 = jnp.zeros((1, tm), jnp.int32)
        for j in range(1, EXP_PER_GROUP):
            better = r[j] > best
            best = jnp.where(better, r[j], best)
            i1 = jnp.where(better, j, i1)
        second = jnp.full((1, tm), -jnp.inf, f32)
        i2 = jnp.zeros((1, tm), jnp.int32)
        for j in range(EXP_PER_GROUP):
            cand = jnp.where(i1 == j, -jnp.inf, r[j])
            better = cand > second
            second = jnp.where(better, cand, second)
            i2 = jnp.where(better, j, i2)
        return pair, i1 + g * EXP_PER_GROUP, i2 + g * EXP_PER_GROUP

    gs, e1, e2 = top2_in_group(0)
    for g in range(1, N_GROUPS):
        gs_g, e1_g, e2_g = top2_in_group(g)
        better = gs_g > gs
        gs = jnp.where(better, gs_g, gs)
        e1 = jnp.where(better, e1_g, e1)
        e2 = jnp.where(better, e2_g, e2)

    eidx = lax.broadcasted_iota(jnp.int32, (N_EXPERTS, tm), 0)
    hit1 = eidx == e1
    hit2 = eidx == e2
    s1 = jnp.sum(jnp.where(hit1, score, 0.0), axis=0, keepdims=True)
    s2 = jnp.sum(jnp.where(hit2, score, 0.0), axis=0, keepdims=True)
    tot = s1 + s2
    e12_ref[...] = jnp.concatenate([e1, e2], axis=0)
    wk_pad = jnp.concatenate([s1 / tot, s2 / tot, jnp.zeros((LANES - 2, tm), f32)], axis=0)
    wk_ref[...] = wk_pad.T


def merge(x_all, ya_all, yb_all, gate_all, w_oa, w_ob, w_o, mod_l, norm2_g, wr_hi, wr_lo, router_bias,
          *, n_rows, n_lat_tiles_per_seq, n_lat_tiles, batch, tm):
    n_tiles = n_rows // tm

    def seg(i):
        return jnp.where(i < n_lat_tiles, i // n_lat_tiles_per_seq, batch)

    row = lambda w: pl.BlockSpec((tm, w), lambda i: (i, 0))
    full = lambda a, b: pl.BlockSpec((a, b), lambda i: (0, 0))
    modspec = lambda j: pl.BlockSpec((None, 1, D_MODEL), lambda i: (seg(i), 0, j))
    vmem = 2 * (2 * W_Z * D_MODEL * 2 + D_MODEL * D_MODEL * 2) + 2 * tm * (
        D_MODEL * 4 * 2 + 2 * W_Z * 2 + W_GATE * 2 + D_MODEL * 2 + LANES * 4) + 8 * tm * D_MODEL * 4
    return pl.pallas_call(
        _merge_kernel,
        out_shape=(
            jax.ShapeDtypeStruct((n_rows, D_MODEL), f32),
            jax.ShapeDtypeStruct((n_rows, D_MODEL // 2), jnp.uint32),
            jax.ShapeDtypeStruct((2, n_rows), jnp.int32),
            jax.ShapeDtypeStruct((n_rows, LANES), f32),
        ),
        grid=(n_tiles,),
        in_specs=[
            row(D_MODEL), row(W_Z), row(W_QB), row(W_GATE),
            full(W_Z, D_MODEL), full(W_QB, D_MODEL), full(D_MODEL, D_MODEL),
            modspec(2), full(1, D_MODEL), modspec(3), modspec(4),
            full(N_EXPERTS, D_MODEL), full(N_EXPERTS, D_MODEL), full(N_EXPERTS, 1),
        ],
        out_specs=(row(D_MODEL), row(D_MODEL // 2), pl.BlockSpec((2, tm), lambda i: (0, i)), row(LANES)),
        compiler_params=pltpu.CompilerParams(
            dimension_semantics=("parallel",), vmem_limit_bytes=_vmem_limit(vmem)),
        name="merge",
    )(x_all, ya_all, yb_all, gate_all, w_oa, w_ob, w_o, mod_l, norm2_g.reshape(1, D_MODEL), mod_l, mod_l,
      wr_hi, wr_lo, router_bias.reshape(N_EXPERTS, 1))


def _moe_kernel(ord_ref, tok_ref, st_ref, x1_ref, hp_ref, wk_ref, wg_ref, wu_ref, wd_ref, g2_ref, fg_ref, o_ref,
                slot_s, xg_s, yp_s, *, tm, final_norm):
    e = pl.program_id(1)
    half = D_MODEL // 2
    start = st_ref[0, e]
    count = st_ref[0, e + 1] - start

    @pl.when(e == 0)
    def _():
        xg_s[...] = jnp.zeros_like(xg_s)

    def chunk(c, carry):
        base = start + c * MOE_CHUNK
        ngroups = (jnp.minimum(count - c * MOE_CHUNK, MOE_CHUNK) + SUBLANES - 1) // SUBLANES

        def gather(g, carry2):
            for u in range(SUBLANES):
                r = g * SUBLANES + u
                xg_s[pl.ds(r, 1), :] = hp_ref[pl.ds(tok_ref[0, base + r], 1), :]
            return carry2

        lax.fori_loop(0, ngroups, gather, 0)
        x_lo, x_hi = _unpack_halves(xg_s[...])
        x_lo = x_lo.astype(bf16)
        x_hi = x_hi.astype(bf16)
        a = _dot(x_lo, wg_ref[0:half, :]) + _dot(x_hi, wg_ref[half:D_MODEL, :])
        b = _dot(x_lo, wu_ref[0:half, :]) + _dot(x_hi, wu_ref[half:D_MODEL, :])
        he = (_silu(a) * b).astype(bf16)
        y = _dot(he, wd_ref[...])
        yp_s[...] = _pack_halves(y)

        def scatter(g, carry2):
            for u in range(SUBLANES):
                r = g * SUBLANES + u
                slot_s[pl.ds(ord_ref[0, base + r], 1), :] = yp_s[pl.ds(r, 1), :]
            return carry2

        lax.fori_loop(0, ngroups, scatter, 0)
        return carry

    lax.fori_loop(0, (count + MOE_CHUNK - 1) // MOE_CHUNK, chunk, 0)

    @pl.when(e == N_EXPERTS - 1)
    def _():
        sub = 256
        lane = lax.broadcasted_iota(jnp.int32, (sub, LANES), 1)
        for r in range(tm // sub):
            rows = pl.ds(r * sub, sub)
            wk = wk_ref[rows, :]
            w1 = jnp.sum(jnp.where(lane == 0, wk, 0.0), axis=-1, keepdims=True)
            w2 = jnp.sum(jnp.where(lane == 1, wk, 0.0), axis=-1, keepdims=True)
            lo1, hi1 = _unpack_halves(slot_s[pl.ds(r * sub, sub), :])
            lo2, hi2 = _unpack_halves(slot_s[pl.ds(tm + r * sub, sub), :])
            acc = jnp.concatenate([w1 * lo1 + w2 * lo2, w1 * hi1 + w2 * hi2], axis=1)
            x2 = x1_ref[rows, :] + g2_ref[...] * acc
            if final_norm:
                ms = jnp.mean(x2 * x2, axis=-1, keepdims=True)
                x2 = x2 * lax.rsqrt(ms + EPS) * fg_ref[...]
            o_ref[rows, :] = x2


def moe(x1, hp, e12, wk, w_gate, w_up, w_down, mod_l, final_g, *, n_lat_tiles_per_seq, n_lat_tiles, batch, tm,
        final_norm):
    n_rows = x1.shape[0]
    n_tiles = n_rows // tm

    ids = e12.reshape(2, n_tiles, tm).transpose(1, 0, 2).reshape(n_tiles, 2 * tm)
    order = jnp.argsort(ids, axis=1, stable=True).astype(jnp.int32)
    tok = jnp.pad(jnp.where(order >= tm, order - tm, order), ((0, 0), (0, MOE_CHUNK)))
    order = jnp.pad(order, ((0, 0), (0, MOE_CHUNK)), constant_values=2 * tm)
    counts = jnp.sum((ids[:, :, None] == jnp.arange(N_EXPERTS, dtype=jnp.int32)[None, None, :]).astype(jnp.int32),
                     axis=1)
    starts = jnp.concatenate([jnp.zeros((n_tiles, 1), jnp.int32), jnp.cumsum(counts, axis=1)], axis=1)
    starts = jnp.pad(starts, ((0, 0), (0, 2 * N_EXPERTS - (N_EXPERTS + 1))))

    def seg(i):
        return jnp.where(i < n_lat_tiles, i // n_lat_tiles_per_seq, batch)

    row = lambda w: pl.BlockSpec((tm, w), lambda i, e: (i, 0))
    smem_row = lambda w: pl.BlockSpec((None, 1, w), lambda i, e: (i, 0, 0), memory_space=pltpu.SMEM)
    kern = functools.partial(_moe_kernel, tm=tm, final_norm=final_norm)
    vmem = (2 * 3 * D_MODEL * D_EXPERT * 2 + 2 * tm * (D_MODEL * 4 * 2 + D_MODEL * 2 + LANES * 4)
            + 2 * tm * D_MODEL * 2 + 16 * 1024 * 1024)
    return pl.pallas_call(
        kern,
        out_shape=jax.ShapeDtypeStruct((n_rows, D_MODEL), f32),
        grid=(n_tiles, N_EXPERTS),
        in_specs=[
            smem_row(2 * tm + MOE_CHUNK), smem_row(2 * tm + MOE_CHUNK), smem_row(2 * N_EXPERTS),
            row(D_MODEL), row(D_MODEL // 2), row(LANES),
            pl.BlockSpec((None, D_MODEL, D_EXPERT), lambda i, e: (e, 0, 0)),
            pl.BlockSpec((None, D_MODEL, D_EXPERT), lambda i, e: (e, 0, 0)),
            pl.BlockSpec((None, D_EXPERT, D_MODEL), lambda i, e: (e, 0, 0)),
            pl.BlockSpec((None, 1, D_MODEL), lambda i, e: (seg(i), 0, 5)),
            pl.BlockSpec((1, D_MODEL), lambda i, e: (0, 0)),
        ],
        out_specs=row(D_MODEL),
        scratch_shapes=[
            pltpu.VMEM((2 * tm + SUBLANES, D_MODEL // 2), jnp.uint32),
            pltpu.VMEM((MOE_CHUNK, D_MODEL // 2), jnp.uint32),
            pltpu.VMEM((MOE_CHUNK, D_MODEL // 2), jnp.uint32),
        ],
        compiler_params=pltpu.CompilerParams(
            dimension_semantics=("parallel", "arbitrary"), vmem_limit_bytes=_vmem_limit(vmem)),
        name="moe",
    )(order.reshape(n_tiles, 1, 2 * tm + MOE_CHUNK), tok.reshape(n_tiles, 1, 2 * tm + MOE_CHUNK),
      starts.reshape(n_tiles, 1, 2 * N_EXPERTS), x1, hp, wk,
      w_gate, w_up, w_down, mod_l, final_g.reshape(1, D_MODEL))


def _prep_w_in(w_in_l):
    qkv_a = 3 * H_A * DK_A
    o_z = qkv_a
    o_b = o_z + H_A * DV_A
    o_a = o_b + 2 * H_A
    o_q = o_a + 2 * H_A
    o_k = o_q + HQ_B * DH_B
    o_v = o_k + HKV_B * DH_B
    o_ga = o_v + HKV_B * DH_B
    w_bg = jnp.zeros((D_MODEL, W_BG), f32).at[:, :4 * H_A].set(w_in_l[:, o_b:o_q])
    wq = w_in_l[:, o_q:o_k].reshape(D_MODEL, HQ_B, DH_B)[:, np.array(HEAD_PERM), :].reshape(D_MODEL, W_QB)
    wq = wq * (DH_B ** -0.5)
    cat = jnp.concatenate(
        [w_in_l[:, :qkv_a], w_in_l[:, o_z:o_b], w_bg, wq, w_in_l[:, o_k:o_v], w_in_l[:, o_v:o_ga],
         w_in_l[:, o_ga:]], axis=1)
    return cat.astype(bf16)


def _rope_tables(seq, tm):
    t = np.arange(seq)
    rows = (t // GRID_W).astype(np.float32)
    cols = (t % GRID_W).astype(np.float32)
    nf = DH_B // 4
    inv = jnp.asarray(ROPE_THETA, f32) ** (-jnp.arange(nf, dtype=f32) / nf)
    ang_r = jnp.asarray(rows)[:, None] * inv[None, :]
    ang_c = jnp.asarray(cols)[:, None] * inv[None, :]
    cos64 = jnp.concatenate([jnp.cos(ang_r), jnp.cos(ang_r), jnp.cos(ang_c), jnp.cos(ang_c)], axis=1)
    zero = jnp.zeros_like(ang_r)
    sa64 = jnp.concatenate([-jnp.sin(ang_r), zero, -jnp.sin(ang_c), zero], axis=1)
    sb64 = jnp.concatenate([zero, jnp.sin(ang_r), zero, jnp.sin(ang_c)], axis=1)

    def fin(tab, fill):
        tab = jnp.concatenate([tab, tab], axis=1)
        return jnp.concatenate([tab, jnp.full((tm, LANES), fill, f32)], axis=0)

    return fin(cos64, 1.0), fin(sa64, 0.0), fin(sb64, 0.0)


TM_TOKEN = 256
TM_MERGE = 512
TM_MOE = 1024


def kernel(x, c, ctx, c_ctx, w_mod, b_mod, norm1_g, norm2_g, w_in, conv_w, a_log, dt_bias, gdn_norm_g, sink,
           w_oa, w_ob, w_o, w_router, router_bias, w_gate, w_up, w_down, final_g):
    batch, seq, d = x.shape
    ctx_len = ctx.shape[1]
    depth = w_mod.shape[0]
    assert d == D_MODEL and seq % SUPER == 0 and ctx_len % SUPER == 0 and seq % TM_MOE == 0
    t_lat = batch * seq
    t_ctx = batch * ctx_len
    assert t_ctx % TM_MOE == 0

    x_all = jnp.concatenate([x.reshape(t_lat, d), ctx.reshape(t_ctx, d)], axis=0)
    t_all = t_lat + t_ctx

    n_cond = ((batch + 1 + SUBLANES - 1) // SUBLANES) * SUBLANES
    cond = jnp.zeros((n_cond, d), f32).at[:batch].set(c).at[batch].set(c_ctx)
    mod = adaln_all(cond, w_mod, b_mod).reshape(depth, n_cond, 1, 6 * d)

    rope_tabs = _rope_tables(seq, TM_TOKEN)
    wr_hi = w_router.T.astype(bf16)
    wr_lo = (w_router.T - wr_hi.astype(f32)).astype(bf16)
    perm_rows = np.concatenate([np.arange(h * DH_B, (h + 1) * DH_B) for h in HEAD_PERM])
    zero_state = jnp.zeros((batch, H_A, DK_A, DV_A), f32)

    tok = dict(n_lat_tiles_per_seq=seq // TM_TOKEN, n_lat_tiles=t_lat // TM_TOKEN, batch=batch, tm=TM_TOKEN)
    tok_merge = dict(n_lat_tiles_per_seq=seq // TM_MERGE, n_lat_tiles=t_lat // TM_MERGE, batch=batch, tm=TM_MERGE)
    tok_moe = dict(n_lat_tiles_per_seq=seq // TM_MOE, n_lat_tiles=t_lat // TM_MOE, batch=batch, tm=TM_MOE)

    for l in range(depth):
        need_ctx = l < depth - 1
        w_cat = _prep_w_in(w_in[l])
        qkv_all, z_all, bg_all, qb_all, kv_all, gate_all = inproj(
            x_all, mod[l], norm1_g[l], w_cat, rope_tabs, a_log[l], dt_bias[l], **tok)

        ya_all = jnp.zeros((t_all, W_Z), bf16)
        ya_all, s_f, s_b = gdn(qkv_all, bg_all, z_all, ya_all, conv_w[l], gdn_norm_g[l], zero_state, zero_state,
                               batch=batch, seq=ctx_len, row_block0=t_lat // ctx_len)
        ya_all, _, _ = gdn(qkv_all, bg_all, z_all, ya_all, conv_w[l], gdn_norm_g[l], s_f, s_b,
                           batch=batch, seq=seq, row_block0=0)

        yb_all = attention(qb_all, kv_all, sink[l], batch=batch, seq=seq, ctx_len=ctx_len,
                           with_ctx_queries=need_ctx)

        n_rows = t_all if need_ctx else t_lat
        x1, hp, e12, wk = merge(
            x_all, ya_all, yb_all, gate_all, w_oa[l].astype(bf16), w_ob[l][perm_rows].astype(bf16),
            w_o[l].astype(bf16), mod[l], norm2_g[l], wr_hi, wr_lo, router_bias, n_rows=n_rows, **tok_merge)
        x_all = moe(x1, hp, e12, wk, w_gate[l].astype(bf16), w_up[l].astype(bf16), w_down[l].astype(bf16),
                    mod[l], final_g, final_norm=not need_ctx, **tok_moe)

    return x_all[:t_lat].reshape(batch, seq, d)
```

```python
import functools

import jax
import jax.numpy as jnp
import numpy as np
from jax import lax
from jax.experimental import pallas as pl
from jax.experimental.pallas import tpu as pltpu

f32 = jnp.float32
bf16 = jnp.bfloat16

D_MODEL = 1024
GRID_W = 64
EPS = 1e-6
H_A = 4
DK_A = 128
DV_A = 128
CONV_K = 5
CHUNK = 64
HQ_B = 8
HKV_B = 2
DH_B = 64
WINDOW = 128
BLK = 128
ROPE_THETA = 10000.0
N_EXPERTS = 16
N_GROUPS = 4
EXP_PER_GROUP = 4
D_EXPERT = 512

LANES = 128
SUBLANES = 8
BF16_ROWS = 16
VMEM_BYTES_V7X = 64 * 1024 * 1024

W_QKV = 3 * H_A * DK_A
W_Z = H_A * DV_A
W_BG = LANES
W_QB = HQ_B * DH_B
W_KV = 2 * HKV_B * DH_B
W_GATE = 2 * D_MODEL
OFF_Z = W_QKV
OFF_BG = OFF_Z + W_Z
OFF_QB = OFF_BG + W_BG
OFF_KV = OFF_QB + W_QB
OFF_GATE = OFF_KV + W_KV
W_TOTAL = OFF_GATE + W_GATE
W_ROPE = W_QB + HKV_B * DH_B

NEG_BIG = -1e30
HEAD_PERM = (0, 4, 1, 5, 2, 6, 3, 7)

SUPER = 4 * CHUNK
HALF = CHUNK // 2
MOE_CHUNK = 192
MOE_CHUNK_SHORT = 128


def _vmem_limit(nbytes):
    return int(min(max(nbytes, 16 * 1024 * 1024), VMEM_BYTES_V7X - 8 * 1024 * 1024))


def _split3(x):
    hi = x.astype(bf16)
    r1 = x - hi.astype(f32)
    mid = r1.astype(bf16)
    lo = (r1 - mid.astype(f32)).astype(bf16)
    return hi, mid, lo


def _dot(a, b):
    return jnp.dot(a, b, preferred_element_type=f32)


def _dot_nt(a, b):
    return lax.dot_general(a, b, (((1,), (1,)), ((), ())), preferred_element_type=f32)


def _dot_exact_lhs_mask(mask_bf, parts):
    return _dot(mask_bf, parts[0]) + _dot(mask_bf, parts[1]) + _dot(mask_bf, parts[2])


def _pack_halves(x):
    w = x.shape[1] // 2
    lo = lax.bitcast_convert_type(x[:, 0:w].astype(bf16).astype(f32), jnp.uint32)
    hi = lax.bitcast_convert_type(x[:, w:2 * w].astype(bf16).astype(f32), jnp.uint32)
    return lax.shift_right_logical(lo, jnp.uint32(16)) | (hi & jnp.uint32(0xFFFF0000))


def _unpack_halves(p):
    lo = lax.bitcast_convert_type(lax.shift_left(p, jnp.uint32(16)), f32)
    hi = lax.bitcast_convert_type(p & jnp.uint32(0xFFFF0000), f32)
    return lo, hi


def _sigmoid(x):
    return 1.0 / (1.0 + jnp.exp(-x))


def _silu(x):
    return x * _sigmoid(x)


def _softplus(x):
    return jnp.maximum(x, 0.0) + jnp.log1p(jnp.exp(-jnp.abs(x)))


def _adaln_kernel(c_ref, w_ref, b_ref, o_ref):
    c = c_ref[...]
    s = _silu(c).astype(bf16)
    o_ref[...] = _dot(s, w_ref[...].astype(bf16)) + b_ref[...]


def adaln_all(cond, w_mod, b_mod):
    depth = w_mod.shape[0]
    r = cond.shape[0]
    nblk = 6
    return pl.pallas_call(
        _adaln_kernel,
        out_shape=jax.ShapeDtypeStruct((depth, r, 6 * D_MODEL), f32),
        grid=(depth, nblk),
        in_specs=[
            pl.BlockSpec((r, D_MODEL), lambda l, j: (0, 0)),
            pl.BlockSpec((None, D_MODEL, D_MODEL), lambda l, j: (l, 0, j)),
            pl.BlockSpec((None, 1, D_MODEL), lambda l, j: (l, 0, j)),
        ],
        out_specs=pl.BlockSpec((None, r, D_MODEL), lambda l, j: (l, 0, j)),
        compiler_params=pltpu.CompilerParams(
            dimension_semantics=("parallel", "parallel"),
            vmem_limit_bytes=_vmem_limit(24 * 1024 * 1024)),
        name="adaln",
    )(cond, w_mod, b_mod.reshape(depth, 1, 6 * D_MODEL))


def _inproj_kernel(x_ref, g_ref, sh_ref, sc_ref, cos_ref, sa_ref, sb_ref, alog_ref, dtb_ref, w_ref,
                   qkv_ref, z_ref, bg_ref, qb_ref, kv_ref, gate_ref):
    x = x_ref[...]
    ms = jnp.mean(x * x, axis=-1, keepdims=True)
    y = x * lax.rsqrt(ms + EPS) * g_ref[...]
    h = (y * (1.0 + sc_ref[...]) + sh_ref[...]).astype(bf16)

    qkv_ref[...] = _dot(h, w_ref[:, 0:W_QKV]).astype(bf16)
    z_ref[...] = _dot(h, w_ref[:, OFF_Z:OFF_Z + W_Z]).astype(bf16)
    raw = _dot(h, w_ref[:, OFF_BG:OFF_BG + W_BG])
    lane = lax.broadcasted_iota(jnp.int32, raw.shape, 1)
    bg_ref[...] = jnp.where(lane < 2 * H_A, _sigmoid(raw),
                            -jnp.exp(alog_ref[...]) * _softplus(raw + dtb_ref[...]))

    pr = _dot(h, w_ref[:, OFF_QB:OFF_QB + W_ROPE])
    reps = W_ROPE // LANES
    cos = jnp.concatenate([cos_ref[...]] * reps, axis=1)
    sa = jnp.concatenate([sa_ref[...]] * reps, axis=1)
    sb = jnp.concatenate([sb_ref[...]] * reps, axis=1)
    half = DH_B // 4
    rot = pr * cos + pltpu.roll(pr, W_ROPE - half, 1) * sa + pltpu.roll(pr, half, 1) * sb
    qb_ref[...] = rot[:, 0:W_QB].astype(bf16)
    kv_ref[:, 0:LANES] = rot[:, W_QB:W_ROPE].astype(bf16)
    kv_ref[:, LANES:2 * LANES] = _dot(h, w_ref[:, OFF_KV + LANES:OFF_KV + 2 * LANES]).astype(bf16)

    gate_ref[...] = _sigmoid(_dot(h, w_ref[:, OFF_GATE:OFF_GATE + W_GATE])).astype(bf16)


def inproj(x_all, mod_l, norm_g, w_cat, rope_tabs, a_log_l, dt_bias_l, *, n_lat_tiles_per_seq, n_lat_tiles, batch,
           tm):
    t_all = x_all.shape[0]
    n_tiles = t_all // tm
    cos_t, sa_t, sb_t = rope_tabs
    pad_lanes = lambda v: jnp.zeros((1, LANES), f32).at[0, 2 * H_A:4 * H_A].set(v.reshape(-1))
    alog_v = pad_lanes(a_log_l)
    dtb_v = pad_lanes(dt_bias_l)

    def seg(i):
        return jnp.where(i < n_lat_tiles, i // n_lat_tiles_per_seq, batch)

    def pos(i):
        return jnp.where(i < n_lat_tiles, i % n_lat_tiles_per_seq, n_lat_tiles_per_seq)

    row = lambda w: pl.BlockSpec((tm, w), lambda i: (i, 0))
    tab = pl.BlockSpec((tm, LANES), lambda i: (pos(i), 0))
    out_shapes = (
        jax.ShapeDtypeStruct((t_all, W_QKV), bf16),
        jax.ShapeDtypeStruct((t_all, W_Z), bf16),
        jax.ShapeDtypeStruct((t_all, W_BG), f32),
        jax.ShapeDtypeStruct((t_all, W_QB), bf16),
        jax.ShapeDtypeStruct((t_all, W_KV), bf16),
        jax.ShapeDtypeStruct((t_all, W_GATE), bf16),
    )
    vmem = 2 * (W_TOTAL * D_MODEL * 2) + 2 * tm * (D_MODEL * 4 + W_TOTAL * 2 + 3 * LANES * 4) + tm * W_TOTAL * 4
    return pl.pallas_call(
        _inproj_kernel,
        out_shape=out_shapes,
        grid=(n_tiles,),
        in_specs=[
            row(D_MODEL),
            pl.BlockSpec((1, D_MODEL), lambda i: (0, 0)),
            pl.BlockSpec((None, 1, D_MODEL), lambda i: (seg(i), 0, 0)),
            pl.BlockSpec((None, 1, D_MODEL), lambda i: (seg(i), 0, 1)),
            tab, tab, tab,
            pl.BlockSpec((1, LANES), lambda i: (0, 0)),
            pl.BlockSpec((1, LANES), lambda i: (0, 0)),
            pl.BlockSpec((D_MODEL, W_TOTAL), lambda i: (0, 0)),
        ],
        out_specs=(row(W_QKV), row(W_Z), row(W_BG), row(W_QB), row(W_KV), row(W_GATE)),
        compiler_params=pltpu.CompilerParams(
            dimension_semantics=("parallel",), vmem_limit_bytes=_vmem_limit(vmem)),
        name="inproj",
    )(x_all, norm_g.reshape(1, D_MODEL), mod_l, mod_l, cos_t, sa_t, sb_t, alog_v, dtb_v, w_cat)


def _gdn_kernel(q_ref, k_ref, v_ref, bg_ref, z_ref, cwq_ref, cwk_ref, cwv_ref, gn_ref, s0f_ref, s0b_ref,
                yin_ref, y_ref, sf_ref, sb_ref,
                u_s, w_s, qg_s, ak_s, gl_s, o_s, st_s, a2_s, x_s, *, seq, nsuper):
    del yin_ref
    hd = pl.program_id(1)
    nch = seq // CHUNK
    cps = SUPER // CHUNK

    ii = lax.broadcasted_iota(jnp.int32, (SUPER, SUPER), 0)
    jj = lax.broadcasted_iota(jnp.int32, (SUPER, SUPER), 1)
    same = (ii // CHUNK) == (jj // CHUNK)
    tri = (same & (jj <= ii), same & (jj >= ii))
    tri_s = (same & (jj < ii), same & (jj > ii))
    cum2_bf = jnp.concatenate([jnp.where(m, 1.0, 0.0) for m in tri], axis=0).astype(bf16)
    lane = lax.broadcasted_iota(jnp.int32, (SUPER, LANES), 1)

    def conv_silu(ref, cw_ref, base, st):
        main = ref[pl.ds(base, SUPER), :].astype(f32)
        p0 = pl.multiple_of(jnp.maximum(base - BF16_ROWS, 0), BF16_ROWS)
        n0 = pl.multiple_of(jnp.minimum(base + SUPER, seq - BF16_ROWS), BF16_ROWS)
        prev = ref[pl.ds(p0, BF16_ROWS), :].astype(f32) * jnp.where(st > 0, 1.0, 0.0)
        nxt = ref[pl.ds(n0, BF16_ROWS), :].astype(f32) * jnp.where(st < nsuper - 1, 1.0, 0.0)
        win = jnp.concatenate([prev, main, nxt], axis=0)
        acc = jnp.zeros((SUPER, LANES), f32)
        for i in range(CONV_K):
            lo_row = BF16_ROWS - CONV_K // 2 + i
            acc = acc + win[lo_row:lo_row + SUPER, :] * cw_ref[i:i + 1, :]
        return _silu(acc)

    def column(x, c):
        col = jnp.sum(jnp.where(lane == c, x, 0.0), axis=-1, keepdims=True)
        return jnp.broadcast_to(col, (SUPER, LANES))

    def front(st):
        base = pl.multiple_of(st * SUPER, SUPER)
        q = conv_silu(q_ref, cwq_ref, base, st)
        yield
        k = conv_silu(k_ref, cwk_ref, base, st)
        yield
        v = conv_silu(v_ref, cwv_ref, base, st)
        yield
        q = q * lax.rsqrt(jnp.sum(q * q, axis=-1, keepdims=True) + EPS) * (DK_A ** -0.5)
        k = k * lax.rsqrt(jnp.sum(k * k, axis=-1, keepdims=True) + EPS)
        q_bf = q.astype(bf16)
        k_bf = k.astype(bf16)
        qkk = _dot_nt(jnp.concatenate([q_bf, k_bf], axis=0), k_bf)
        qk_raw = qkk[0:SUPER]
        kk_raw = qkk[SUPER:2 * SUPER]
        bg = bg_ref[pl.ds(base, SUPER), :]
        rows = pl.ds(base, SUPER)
        yield

        g_both = [column(bg, 2 * H_A + d * H_A + hd) for d in range(2)]
        parts = [jnp.concatenate(p, axis=1) for p in zip(_split3(g_both[0]), _split3(g_both[1]))]
        csum_both = _dot_exact_lhs_mask(cum2_bf, parts)
        yield

        for d in range(2):
            beta = column(bg, d * H_A + hd)
            mask, mask_s = tri[d], tri_s[d]
            csum = csum_both[d * SUPER:(d + 1) * SUPER, d * LANES:(d + 1) * LANES]
            ct = jnp.concatenate([csum[0:LANES].T, csum[LANES:2 * LANES].T], axis=1)
            ct = jnp.concatenate([ct, ct], axis=0)
            c2 = jnp.concatenate([csum, csum], axis=1)
            decay = jnp.where(mask, jnp.exp(jnp.where(mask, c2 - ct, 0.0)), 0.0)
            yield
            beta2 = jnp.concatenate([beta, beta], axis=1)
            a_mat = jnp.where(mask_s, beta2 * kk_raw * decay, 0.0)
            if d == 1:
                a_mat = a_mat.T
            for c in range(cps):
                cs = slice(c * CHUNK, (c + 1) * CHUNK)
                blk = a_mat[cs, cs]
                prob = d * nch + st * cps + c
                a2_s[pl.ds(pl.multiple_of(prob * HALF, HALF), HALF), :] = jnp.concatenate(
                    [blk[0:HALF], blk[HALF:CHUNK]], axis=1)
            attn = jnp.where(mask, qk_raw * decay, 0.0)
            yield

            last = (CHUNK - 1, 0)[d]
            tot = jnp.concatenate(
                [jnp.broadcast_to(csum[c * CHUNK + last:c * CHUNK + last + 1, :], (CHUNK, LANES))
                 for c in range(cps)], axis=0)
            egc = jnp.exp(csum)
            u_s[d, rows, :] = (v * beta).astype(bf16)
            w_s[d, rows, :] = (k * beta * egc).astype(bf16)
            qg_s[d, rows, :] = (q * egc).astype(bf16)
            yield
            kg_t = (k * jnp.exp(tot - csum)).T
            glast = jnp.exp(tot)
            for c in range(cps):
                cs = slice(c * CHUNK, (c + 1) * CHUNK)
                ak_s[d, st * cps + c] = jnp.concatenate([attn[cs, cs], kg_t[:, cs]], axis=0).astype(bf16)
                gl_s[d, pl.ds(st * cps + c, 1), :] = glast[c * CHUNK:c * CHUNK + 1, :]
            yield

    def back(st):
        rows = pl.ds(pl.multiple_of(st * SUPER, SUPER), SUPER)
        lane2 = lax.broadcasted_iota(jnp.int32, (CHUNK, SUPER), 1)
        for d in range(2):
            blocks = []
            for c in range(cps):
                prob = d * nch + st * cps + c
                half = a2_s[pl.ds(pl.multiple_of(prob * HALF, HALF), HALF), :]
                blk = jnp.concatenate([half[:, 0:CHUNK], half[:, CHUNK:2 * CHUNK]], axis=0)
                wide = jnp.concatenate([blk] * cps, axis=1)
                blocks.append(jnp.where(lane2 // CHUNK == c, wide, 0.0))
            t_m = jnp.concatenate(blocks, axis=0)
            if d == 1:
                t_m = t_m.T
            yield
            rhs = jnp.concatenate([u_s[d, rows, :], w_s[d, rows, :]], axis=1)
            sol = rhs.astype(f32) + _dot(t_m.astype(bf16), rhs)
            u_s[d, rows, :] = sol[:, 0:DV_A].astype(bf16)
            w_s[d, rows, :] = sol[:, DV_A:2 * DV_A].astype(bf16)
            yield

    def interleave(*gens):
        active = list(gens)
        while active:
            for g in list(active):
                if next(g, StopIteration) is StopIteration:
                    active.remove(g)

    nprob = 2 * nch
    if nprob < LANES:
        a2_s[...] = jnp.zeros_like(a2_s)

    def front_loop(st, carry):
        interleave(front(st))
        return carry

    lax.fori_loop(0, nsuper, front_loop, 0)

    for r in range(HALF):
        xt = a2_s[pl.ds(r, LANES, stride=HALF), :].T
        x_s[r] = xt[0:CHUNK]
        x_s[r + HALF] = xt[CHUNK:2 * CHUNK]

    nblk = CHUNK // SUBLANES
    for i in range(1, CHUNK):
        acc = [-x_s[i, jb * SUBLANES:(jb + 1) * SUBLANES, :] for jb in range(nblk)]
        for k in range(1, i):
            a_ik = x_s[i, k:k + 1, :]
            for jb in range((k - 1) // SUBLANES + 1):
                acc[jb] = acc[jb] - a_ik * x_s[k, jb * SUBLANES:(jb + 1) * SUBLANES, :]
        for jb in range(nblk):
            x_s[i, jb * SUBLANES:(jb + 1) * SUBLANES, :] = acc[jb]

    for r in range(HALF):
        yt = jnp.concatenate([x_s[r], x_s[r + HALF]], axis=0).T
        a2_s[pl.ds(r, LANES, stride=HALF), :] = yt

    def back_loop(st, carry):
        interleave(back(st))
        return carry

    lax.fori_loop(0, nsuper, back_loop, 0)

    st_s[0] = s0f_ref[...]
    st_s[1] = s0b_ref[...]

    def scan_step(n, accumulate):
        for d in range(2):
            c = n if d == 0 else nch - 1 - n
            rows = pl.ds(pl.multiple_of(c * CHUNK, CHUNK), CHUNK)
            s_old = st_s[d]
            ws = _dot(jnp.concatenate([w_s[d, rows, :], qg_s[d, rows, :]], axis=0), s_old.astype(bf16))
            v_new = (u_s[d, rows, :].astype(f32) - ws[0:CHUNK]).astype(bf16)
            ov = _dot(ak_s[d, c], v_new)
            o = ws[CHUNK:2 * CHUNK] + ov[0:CHUNK]
            st_s[d] = s_old * gl_s[d, pl.ds(c, 1), :] + ov[CHUNK:CHUNK + DK_A]
            if accumulate:
                o_s[rows, :] = o_s[rows, :] + o
            else:
                o_s[rows, :] = o

    def scan_a(n, carry):
        scan_step(n, False)
        return carry

    def scan_b(n, carry):
        scan_step(n, True)
        return carry

    lax.fori_loop(0, nch // 2, scan_a, 0)
    lax.fori_loop(nch // 2, nch, scan_b, 0)

    def finish(st, carry):
        rows = pl.ds(pl.multiple_of(st * SUPER, SUPER), SUPER)
        o = o_s[rows, :]
        yn = o * lax.rsqrt(jnp.mean(o * o, axis=-1, keepdims=True) + EPS) * gn_ref[...]
        y_ref[rows, :] = (yn * _silu(z_ref[rows, :].astype(f32))).astype(bf16)
        return carry

    lax.fori_loop(0, nsuper, finish, 0)
    sf_ref[...] = st_s[0]
    sb_ref[...] = st_s[1]


def gdn(qkv_all, bg_all, z_all, ya_all, conv_w, gdn_g, s0f, s0b, *, batch, seq, row_block0):
    nsuper = seq // SUPER
    nch = seq // CHUNK
    nch_pad = max(nch, SUBLANES)
    cw = jnp.zeros((3 * H_A, SUBLANES, LANES), f32).at[:, :CONV_K, :].set(
        conv_w.reshape(CONV_K, 3 * H_A, LANES).transpose(1, 0, 2))

    col = lambda off: pl.BlockSpec((seq, LANES), lambda b, h: (row_block0 + b, off + h))
    cwspec = lambda off: pl.BlockSpec((None, SUBLANES, LANES), lambda b, h: (off + h, 0, 0))
    st_spec = pl.BlockSpec((None, None, DK_A, DV_A), lambda b, h: (b, h, 0, 0))
    kern = functools.partial(_gdn_kernel, seq=seq, nsuper=nsuper)
    vmem = (2 * seq * LANES * (3 * 2 + 2 + 4 + 2) + seq * LANES * (6 * 2 + 6 * 2 + 4)
            + 4 * 4 * DK_A * DV_A * 4 + 20 * 1024 * 1024)
    return pl.pallas_call(
        kern,
        out_shape=(
            jax.ShapeDtypeStruct(ya_all.shape, bf16),
            jax.ShapeDtypeStruct((batch, H_A, DK_A, DV_A), f32),
            jax.ShapeDtypeStruct((batch, H_A, DK_A, DV_A), f32),
        ),
        grid=(batch, H_A),
        in_specs=[
            col(0), col(H_A), col(2 * H_A),
            pl.BlockSpec((seq, LANES), lambda b, h: (row_block0 + b, 0)),
            col(0),
            cwspec(0), cwspec(H_A), cwspec(2 * H_A),
            pl.BlockSpec((1, DV_A), lambda b, h: (0, 0)),
            st_spec, st_spec,
            pl.BlockSpec(memory_space=pl.ANY),
        ],
        out_specs=(col(0), st_spec, st_spec),
        scratch_shapes=[
            pltpu.VMEM((2, seq, LANES), bf16),
            pltpu.VMEM((2, seq, LANES), bf16),
            pltpu.VMEM((2, seq, LANES), bf16),
            pltpu.VMEM((2, nch, CHUNK + DK_A, CHUNK), bf16),
            pltpu.VMEM((2, nch_pad, LANES), f32),
            pltpu.VMEM((seq, LANES), f32),
            pltpu.VMEM((2, DK_A, DV_A), f32),
            pltpu.VMEM((LANES * HALF, LANES), f32),
            pltpu.VMEM((CHUNK, CHUNK, LANES), f32),
        ],
        input_output_aliases={11: 0},
        compiler_params=pltpu.CompilerParams(
            dimension_semantics=("parallel", "parallel"), vmem_limit_bytes=_vmem_limit(vmem)),
        name="gdn",
    )(qkv_all, qkv_all, qkv_all, bg_all, z_all, cw, cw, cw, gdn_g.reshape(1, DV_A), s0f, s0b, ya_all)


def _attn_kernel(sink_ref, q_ref, kp_ref, kc_ref, kn_ref, kx_ref, o_ref, *, n_lat_blocks, blocks_per_seq):
    i = pl.program_id(0)
    is_lat = i < n_lat_blocks
    n = i % blocks_per_seq
    kv = jnp.concatenate([kp_ref[...], kc_ref[...], kn_ref[...], kx_ref[...]], axis=0)
    k2 = kv[:, 0:LANES]
    v2 = kv[:, LANES:2 * LANES]
    nctx = kx_ref.shape[0]

    row = lax.broadcasted_iota(jnp.int32, (BLK, BLK), 0)
    col = lax.broadcasted_iota(jnp.int32, (BLK, BLK), 1)
    ok_prev = (col >= row) & is_lat & (n > 0)
    ok_cent = jnp.broadcast_to(is_lat, (BLK, BLK))
    ok_next = (col <= row) & is_lat & (n < blocks_per_seq - 1)
    zeros = jnp.zeros((BLK, BLK), f32)
    bias = jnp.concatenate(
        [jnp.where(ok_prev, zeros, NEG_BIG), jnp.where(ok_cent, zeros, NEG_BIG),
         jnp.where(ok_next, zeros, NEG_BIG), jnp.zeros((BLK, nctx), f32)], axis=1)
    bias2 = jnp.concatenate([bias, bias], axis=0)

    lane = lax.broadcasted_iota(jnp.int32, (BLK, LANES), 1)
    lo = lane < DH_B
    lo_bf = jnp.where(lo, 1.0, 0.0).astype(bf16)
    hi_bf = jnp.where(lo, 0.0, 1.0).astype(bf16)
    rows2 = lax.broadcasted_iota(jnp.int32, (2 * BLK, 1), 0)
    for p in range(HQ_B // 2):
        q2 = q_ref[:, p * LANES:(p + 1) * LANES]
        qs = jnp.concatenate([q2 * lo_bf, q2 * hi_bf], axis=0)
        s = _dot_nt(qs, k2) + bias2
        sink = jnp.where(rows2 < BLK, sink_ref[p], sink_ref[HQ_B // 2 + p])
        m = jnp.maximum(jnp.max(s, axis=-1, keepdims=True), sink)
        e = jnp.exp(s - m)
        den = jnp.sum(e, axis=-1, keepdims=True) + jnp.exp(sink - m)
        o = _dot(e.astype(bf16), v2) * (1.0 / den)
        o_ref[:, p * LANES:(p + 1) * LANES] = jnp.where(lo, o[0:BLK], o[BLK:2 * BLK]).astype(bf16)


def attention(qb_all, kv_all, sink, *, batch, seq, ctx_len, with_ctx_queries):
    t_lat = batch * seq
    bps = seq // BLK
    n_lat = t_lat // BLK
    cps = ctx_len // BLK
    n_blocks = n_lat + (batch * cps if with_ctx_queries else 0)
    ctx_blk0 = t_lat // ctx_len

    def bidx(i):
        return jnp.where(i < n_lat, i // bps, (i - n_lat) // cps)

    def nbr(delta):
        def f(i):
            n = i % bps
            j = jnp.clip(n + delta, 0, bps - 1)
            return jnp.where(i < n_lat, (i // bps) * bps + j, 0)
        return f

    kern = functools.partial(_attn_kernel, n_lat_blocks=n_lat, blocks_per_seq=bps)
    return pl.pallas_call(
        kern,
        out_shape=jax.ShapeDtypeStruct((n_blocks * BLK, W_QB), bf16),
        grid=(n_blocks,),
        in_specs=[
            pl.BlockSpec(memory_space=pltpu.SMEM),
            pl.BlockSpec((BLK, W_QB), lambda i: (i, 0)),
            pl.BlockSpec((BLK, W_KV), lambda i: (nbr(-1)(i), 0)),
            pl.BlockSpec((BLK, W_KV), lambda i: (nbr(0)(i), 0)),
            pl.BlockSpec((BLK, W_KV), lambda i: (nbr(1)(i), 0)),
            pl.BlockSpec((ctx_len, W_KV), lambda i: (ctx_blk0 + bidx(i), 0)),
        ],
        out_specs=pl.BlockSpec((BLK, W_QB), lambda i: (i, 0)),
        compiler_params=pltpu.CompilerParams(
            dimension_semantics=("parallel",), vmem_limit_bytes=_vmem_limit(32 * 1024 * 1024)),
        name="attn",
    )(sink, qb_all, kv_all, kv_all, kv_all, kv_all)


def _merge_kernel(x_ref, ya_ref, yb_ref, gate_ref, woa_ref, wob_ref, wo_ref, g1_ref, n2_ref, sh2_ref, sc2_ref,
                  wrh_ref, wrl_ref, rb_ref, x1_ref, hp_ref, e12_ref, wk_ref):
    pa = _dot(ya_ref[...], woa_ref[...])
    pb = _dot(yb_ref[...], wob_ref[...])
    m = gate_ref[:, 0:D_MODEL].astype(f32) * pa + gate_ref[:, D_MODEL:2 * D_MODEL].astype(f32) * pb
    y = _dot(m.astype(bf16), wo_ref[...])
    x1 = x_ref[...] + g1_ref[...] * y
    x1_ref[...] = x1
    ms = jnp.mean(x1 * x1, axis=-1, keepdims=True)
    h2 = (x1 * lax.rsqrt(ms + EPS) * n2_ref[...]) * (1.0 + sc2_ref[...]) + sh2_ref[...]
    h_hi = h2.astype(bf16)
    hp_ref[...] = _pack_halves(h_hi.astype(f32))
    h_lo = (h2 - h_hi.astype(f32)).astype(bf16)

    logit = _dot_nt(wrh_ref[...], h_hi) + _dot_nt(wrl_ref[...], h_hi) + _dot_nt(wrh_ref[...], h_lo)
    score = _sigmoid(logit)
    sel = score + rb_ref[...]
    tm = sel.shape[1]

    def top2_in_group(g):
        r = [sel[g * EXP_PER_GROUP + j:g * EXP_PER_GROUP + j + 1, :] for j in range(EXP_PER_GROUP)]
        pair = None
        for a in range(EXP_PER_GROUP):
            for b in range(a + 1, EXP_PER_GROUP):
                s_ab = r[a] + r[b]
                pair = s_ab if pair is None else jnp.maximum(pair, s_ab)
        best = r[0]
        i1 = jnp.zeros((1, tm), jnp.int32)
        for j in range(1, EXP_PER_GROUP):
            better = r[j] > best
            best = jnp.where(better, r[j], best)
            i1 = jnp.where(better, j, i1)
        second = jnp.full((1, tm), -jnp.inf, f32)
        i2 = jnp.zeros((1, tm), jnp.int32)
        for j in range(EXP_PER_GROUP):
            cand = jnp.where(i1 == j, -jnp.inf, r[j])
            better = cand > second
            second = jnp.where(better, cand, second)
            i2 = jnp.where(better, j, i2)
        return pair, i1 + g * EXP_PER_GROUP, i2 + g * EXP_PER_GROUP

    gs, e1, e2 = top2_in_group(0)
    for g in range(1, N_GROUPS):
        gs_g, e1_g, e2_g = top2_in_group(g)
        better = gs_g > gs
        gs = jnp.where(better, gs_g, gs)
        e1 = jnp.where(better, e1_g, e1)
        e2 = jnp.where(better, e2_g, e2)

    eidx = lax.broadcasted_iota(jnp.int32, (N_EXPERTS, tm), 0)
    hit1 = eidx == e1
    hit2 = eidx == e2
    s1 = jnp.sum(jnp.where(hit1, score, 0.0), axis=0, keepdims=True)
    s2 = jnp.sum(jnp.where(hit2, score, 0.0), axis=0, keepdims=True)
    tot = s1 + s2
    e12_ref[...] = jnp.concatenate([e1, e2], axis=0)
    wk_pad = jnp.concatenate([s1 / tot, s2 / tot, jnp.zeros((LANES - 2, tm), f32)], axis=0)
    wk_ref[...] = wk_pad.T


def merge(x_all, ya_all, yb_all, gate_all, w_oa, w_ob, w_o, mod_l, norm2_g, wr_hi, wr_lo, router_bias,
          *, n_rows, n_lat_tiles_per_seq, n_lat_tiles, batch, tm):
    n_tiles = n_rows // tm

    def seg(i):
        return jnp.where(i < n_lat_tiles, i // n_lat_tiles_per_seq, batch)

    row = lambda w: pl.BlockSpec((tm, w), lambda i: (i, 0))
    full = lambda a, b: pl.BlockSpec((a, b), lambda i: (0, 0))
    modspec = lambda j: pl.BlockSpec((None, 1, D_MODEL), lambda i: (seg(i), 0, j))
    vmem = 2 * (2 * W_Z * D_MODEL * 2 + D_MODEL * D_MODEL * 2) + 2 * tm * (
        D_MODEL * 4 * 2 + 2 * W_Z * 2 + W_GATE * 2 + D_MODEL * 2 + LANES * 4) + 8 * tm * D_MODEL * 4
    return pl.pallas_call(
        _merge_kernel,
        out_shape=(
            jax.ShapeDtypeStruct((n_rows, D_MODEL), f32),
            jax.ShapeDtypeStruct((n_rows, D_MODEL // 2), jnp.uint32),
            jax.ShapeDtypeStruct((2, n_rows), jnp.int32),
            jax.ShapeDtypeStruct((n_rows, LANES), f32),
        ),
        grid=(n_tiles,),
        in_specs=[
            row(D_MODEL), row(W_Z), row(W_QB), row(W_GATE),
            full(W_Z, D_MODEL), full(W_QB, D_MODEL), full(D_MODEL, D_MODEL),
            modspec(2), full(1, D_MODEL), modspec(3), modspec(4),
            full(N_EXPERTS, D_MODEL), full(N_EXPERTS, D_MODEL), full(N_EXPERTS, 1),
        ],
        out_specs=(row(D_MODEL), row(D_MODEL // 2), pl.BlockSpec((2, tm), lambda i: (0, i)), row(LANES)),
        compiler_params=pltpu.CompilerParams(
            dimension_semantics=("parallel",), vmem_limit_bytes=_vmem_limit(vmem)),
        name="merge",
    )(x_all, ya_all, yb_all, gate_all, w_oa, w_ob, w_o, mod_l, norm2_g.reshape(1, D_MODEL), mod_l, mod_l,
      wr_hi, wr_lo, router_bias.reshape(N_EXPERTS, 1))


def _moe_kernel(ord_ref, tok_ref, st_ref, x1_ref, hp_ref, wk_ref, wg_ref, wu_ref, wd_ref, g2_ref, fg_ref, o_ref,
                slot_s, xg_s, yp_s, *, tm, final_norm):
    e = pl.program_id(1)
    half = D_MODEL // 2
    start = st_ref[0, e]
    count = st_ref[0, e + 1] - start

    @pl.when(e == 0)
    def _():
        xg_s[...] = jnp.zeros_like(xg_s)

    def chunk(c, carry):
        base = start + c * MOE_CHUNK
        ngroups = (jnp.minimum(count - c * MOE_CHUNK, MOE_CHUNK) + SUBLANES - 1) // SUBLANES

        def gather(g, carry2):
            r0 = pl.multiple_of(g * SUBLANES, SUBLANES)
            rows = [hp_ref[pl.ds(tok_ref[0, base + r0 + u], 1), :] for u in range(SUBLANES)]
            xg_s[pl.ds(r0, SUBLANES), :] = jnp.concatenate(rows, axis=0)
            return carry2

        lax.fori_loop(0, ngroups, gather, 0)

        def ffn(m):
            x_lo, x_hi = _unpack_halves(xg_s[0:m, :])
            x_lo = x_lo.astype(bf16)
            x_hi = x_hi.astype(bf16)
            a = _dot(x_lo, wg_ref[0:half, :]) + _dot(x_hi, wg_ref[half:D_MODEL, :])
            b = _dot(x_lo, wu_ref[0:half, :]) + _dot(x_hi, wu_ref[half:D_MODEL, :])
            he = (_silu(a) * b).astype(bf16)
            yp_s[0:m, :] = _pack_halves(_dot(he, wd_ref[...]))

        short = ngroups * SUBLANES <= MOE_CHUNK_SHORT

        @pl.when(short)
        def _():
            ffn(MOE_CHUNK_SHORT)

        @pl.when(jnp.logical_not(short))
        def _():
            ffn(MOE_CHUNK)

        def scatter(g, carry2):
            r0 = pl.multiple_of(g * SUBLANES, SUBLANES)
            blk = yp_s[pl.ds(r0, SUBLANES), :]
            for u in range(SUBLANES):
                slot_s[pl.ds(ord_ref[0, base + r0 + u], 1), :] = blk[u:u + 1, :]
            return carry2

        lax.fori_loop(0, ngroups, scatter, 0)
        return carry

    lax.fori_loop(0, (count + MOE_CHUNK - 1) // MOE_CHUNK, chunk, 0)

    @pl.when(e == N_EXPERTS - 1)
    def _():
        sub = 256
        lane = lax.broadcasted_iota(jnp.int32, (sub, LANES), 1)
        for r in range(tm // sub):
            rows = pl.ds(r * sub, sub)
            wk = wk_ref[rows, :]
            w1 = jnp.sum(jnp.where(lane == 0, wk, 0.0), axis=-1, keepdims=True)
            w2 = jnp.sum(jnp.where(lane == 1, wk, 0.0), axis=-1, keepdims=True)
            lo1, hi1 = _unpack_halves(slot_s[pl.ds(r * sub, sub), :])
            lo2, hi2 = _unpack_halves(slot_s[pl.ds(tm + r * sub, sub), :])
            acc = jnp.concatenate([w1 * lo1 + w2 * lo2, w1 * hi1 + w2 * hi2], axis=1)
            x2 = x1_ref[rows, :] + g2_ref[...] * acc
            if final_norm:
                ms = jnp.mean(x2 * x2, axis=-1, keepdims=True)
                x2 = x2 * lax.rsqrt(ms + EPS) * fg_ref[...]
            o_ref[rows, :] = x2


def moe(x1, hp, e12, wk, w_gate, w_up, w_down, mod_l, final_g, *, n_lat_tiles_per_seq, n_lat_tiles, batch, tm,
        final_norm):
    n_rows = x1.shape[0]
    n_tiles = n_rows // tm

    ids = e12.reshape(2, n_tiles, tm).transpose(1, 0, 2).reshape(n_tiles, 2 * tm)
    order = jnp.argsort(ids, axis=1, stable=True).astype(jnp.int32)
    tok = jnp.pad(jnp.where(order >= tm, order - tm, order), ((0, 0), (0, MOE_CHUNK)))
    order = jnp.pad(order, ((0, 0), (0, MOE_CHUNK)), constant_values=2 * tm)
    counts = jnp.sum((ids[:, :, None] == jnp.arange(N_EXPERTS, dtype=jnp.int32)[None, None, :]).astype(jnp.int32),
                     axis=1)
    starts = jnp.concatenate([jnp.zeros((n_tiles, 1), jnp.int32), jnp.cumsum(counts, axis=1)], axis=1)
    starts = jnp.pad(starts, ((0, 0), (0, 2 * N_EXPERTS - (N_EXPERTS + 1))))

    def seg(i):
        return jnp.where(i < n_lat_tiles, i // n_lat_tiles_per_seq, batch)

    row = lambda w: pl.BlockSpec((tm, w), lambda i, e: (i, 0))
    smem_row = lambda w: pl.BlockSpec((None, 1, w), lambda i, e: (i, 0, 0), memory_space=pltpu.SMEM)
    kern = functools.partial(_moe_kernel, tm=tm, final_norm=final_norm)
    vmem = (2 * 3 * D_MODEL * D_EXPERT * 2 + 2 * tm * (D_MODEL * 4 * 2 + D_MODEL * 2 + LANES * 4)
            + 2 * tm * D_MODEL * 2 + 16 * 1024 * 1024)
    return pl.pallas_call(
        kern,
        out_shape=jax.ShapeDtypeStruct((n_rows, D_MODEL), f32),
        grid=(n_tiles, N_EXPERTS),
        in_specs=[
            smem_row(2 * tm + MOE_CHUNK), smem_row(2 * tm + MOE_CHUNK), smem_row(2 * N_EXPERTS),
            row(D_MODEL), row(D_MODEL // 2), row(LANES),
            pl.BlockSpec((None, D_MODEL, D_EXPERT), lambda i, e: (e, 0, 0)),
            pl.BlockSpec((None, D_MODEL, D_EXPERT), lambda i, e: (e, 0, 0)),
            pl.BlockSpec((None, D_EXPERT, D_MODEL), lambda i, e: (e, 0, 0)),
            pl.BlockSpec((None, 1, D_MODEL), lambda i, e: (seg(i), 0, 5)),
            pl.BlockSpec((1, D_MODEL), lambda i, e: (0, 0)),
        ],
        out_specs=row(D_MODEL),
        scratch_shapes=[
            pltpu.VMEM((2 * tm + SUBLANES, D_MODEL // 2), jnp.uint32),
            pltpu.VMEM((MOE_CHUNK, D_MODEL // 2), jnp.uint32),
            pltpu.VMEM((MOE_CHUNK, D_MODEL // 2), jnp.uint32),
        ],
        compiler_params=pltpu.CompilerParams(
            dimension_semantics=("parallel", "arbitrary"), vmem_limit_bytes=_vmem_limit(vmem)),
        name="moe",
    )(order.reshape(n_tiles, 1, 2 * tm + MOE_CHUNK), tok.reshape(n_tiles, 1, 2 * tm + MOE_CHUNK),
      starts.reshape(n_tiles, 1, 2 * N_EXPERTS), x1, hp, wk,
      w_gate, w_up, w_down, mod_l, final_g.reshape(1, D_MODEL))


def _prep_w_in(w_in_l):
    qkv_a = 3 * H_A * DK_A
    o_z = qkv_a
    o_b = o_z + H_A * DV_A
    o_a = o_b + 2 * H_A
    o_q = o_a + 2 * H_A
    o_k = o_q + HQ_B * DH_B
    o_v = o_k + HKV_B * DH_B
    o_ga = o_v + HKV_B * DH_B
    w_bg = jnp.zeros((D_MODEL, W_BG), f32).at[:, :4 * H_A].set(w_in_l[:, o_b:o_q])
    wq = w_in_l[:, o_q:o_k].reshape(D_MODEL, HQ_B, DH_B)[:, np.array(HEAD_PERM), :].reshape(D_MODEL, W_QB)
    wq = wq * (DH_B ** -0.5)
    cat = jnp.concatenate(
        [w_in_l[:, :qkv_a], w_in_l[:, o_z:o_b], w_bg, wq, w_in_l[:, o_k:o_v], w_in_l[:, o_v:o_ga],
         w_in_l[:, o_ga:]], axis=1)
    return cat.astype(bf16)


def _rope_tables(seq, tm):
    t = np.arange(seq)
    rows = (t // GRID_W).astype(np.float32)
    cols = (t % GRID_W).astype(np.float32)
    nf = DH_B // 4
    inv = jnp.asarray(ROPE_THETA, f32) ** (-jnp.arange(nf, dtype=f32) / nf)
    ang_r = jnp.asarray(rows)[:, None] * inv[None, :]
    ang_c = jnp.asarray(cols)[:, None] * inv[None, :]
    cos64 = jnp.concatenate([jnp.cos(ang_r), jnp.cos(ang_r), jnp.cos(ang_c), jnp.cos(ang_c)], axis=1)
    zero = jnp.zeros_like(ang_r)
    sa64 = jnp.concatenate([-jnp.sin(ang_r), zero, -jnp.sin(ang_c), zero], axis=1)
    sb64 = jnp.concatenate([zero, jnp.sin(ang_r), zero, jnp.sin(ang_c)], axis=1)

    def fin(tab, fill):
        tab = jnp.concatenate([tab, tab], axis=1)
        return jnp.concatenate([tab, jnp.full((tm, LANES), fill, f32)], axis=0)

    return fin(cos64, 1.0), fin(sa64, 0.0), fin(sb64, 0.0)


TM_TOKEN = 256
TM_MERGE = 512
TM_MOE = 1024


def kernel(x, c, ctx, c_ctx, w_mod, b_mod, norm1_g, norm2_g, w_in, conv_w, a_log, dt_bias, gdn_norm_g, sink,
           w_oa, w_ob, w_o, w_router, router_bias, w_gate, w_up, w_down, final_g):
    batch, seq, d = x.shape
    ctx_len = ctx.shape[1]
    depth = w_mod.shape[0]
    assert d == D_MODEL and seq % SUPER == 0 and ctx_len % SUPER == 0 and seq % TM_MOE == 0
    t_lat = batch * seq
    t_ctx = batch * ctx_len
    assert t_ctx % TM_MOE == 0

    x_all = jnp.concatenate([x.reshape(t_lat, d), ctx.reshape(t_ctx, d)], axis=0)
    t_all = t_lat + t_ctx

    n_cond = ((batch + 1 + SUBLANES - 1) // SUBLANES) * SUBLANES
    cond = jnp.zeros((n_cond, d), f32).at[:batch].set(c).at[batch].set(c_ctx)
    mod = adaln_all(cond, w_mod, b_mod).reshape(depth, n_cond, 1, 6 * d)

    rope_tabs = _rope_tables(seq, TM_TOKEN)
    wr_hi = w_router.T.astype(bf16)
    wr_lo = (w_router.T - wr_hi.astype(f32)).astype(bf16)
    perm_rows = np.concatenate([np.arange(h * DH_B, (h + 1) * DH_B) for h in HEAD_PERM])
    zero_state = jnp.zeros((batch, H_A, DK_A, DV_A), f32)

    tok = dict(n_lat_tiles_per_seq=seq // TM_TOKEN, n_lat_tiles=t_lat // TM_TOKEN, batch=batch, tm=TM_TOKEN)
    tok_merge = dict(n_lat_tiles_per_seq=seq // TM_MERGE, n_lat_tiles=t_lat // TM_MERGE, batch=batch, tm=TM_MERGE)
    tok_moe = dict(n_lat_tiles_per_seq=seq // TM_MOE, n_lat_tiles=t_lat // TM_MOE, batch=batch, tm=TM_MOE)

    for l in range(depth):
        need_ctx = l < depth - 1
        w_cat = _prep_w_in(w_in[l])
        qkv_all, z_all, bg_all, qb_all, kv_all, gate_all = inproj(
            x_all, mod[l], norm1_g[l], w_cat, rope_tabs, a_log[l], dt_bias[l], **tok)

        ya_all = jnp.zeros((t_all, W_Z), bf16)
        ya_all, s_f, s_b = gdn(qkv_all, bg_all, z_all, ya_all, conv_w[l], gdn_norm_g[l], zero_state, zero_state,
                               batch=batch, seq=ctx_len, row_block0=t_lat // ctx_len)
        ya_all, _, _ = gdn(qkv_all, bg_all, z_all, ya_all, conv_w[l], gdn_norm_g[l], s_f, s_b,
                           batch=batch, seq=seq, row_block0=0)

        yb_all = attention(qb_all, kv_all, sink[l], batch=batch, seq=seq, ctx_len=ctx_len,
                           with_ctx_queries=need_ctx)

        n_rows = t_all if need_ctx else t_lat
        x1, hp, e12, wk = merge(
            x_all, ya_all, yb_all, gate_all, w_oa[l].astype(bf16), w_ob[l][perm_rows].astype(bf16),
            w_o[l].astype(bf16), mod[l], norm2_g[l], wr_hi, wr_lo, router_bias, n_rows=n_rows, **tok_merge)
        x_all = moe(x1, hp, e12, wk, w_gate[l].astype(bf16), w_up[l].astype(bf16), w_down[l].astype(bf16),
                    mod[l], final_g, final_norm=not need_ctx, **tok_moe)

    return x_all[:t_lat].reshape(batch, seq, d)
```

```python
import functools

import jax
import jax.numpy as jnp
import numpy as np
from jax import lax
from jax.experimental import pallas as pl
from jax.experimental.pallas import tpu as pltpu

f32 = jnp.float32
bf16 = jnp.bfloat16

D_MODEL = 1024
GRID_W = 64
EPS = 1e-6
H_A = 4
DK_A = 128
DV_A = 128
CONV_K = 5
CHUNK = 64
HQ_B = 8
HKV_B = 2
DH_B = 64
WINDOW = 128
BLK = 128
ROPE_THETA = 10000.0
N_EXPERTS = 16
N_GROUPS = 4
EXP_PER_GROUP = 4
D_EXPERT = 512

LANES = 128
SUBLANES = 8
BF16_ROWS = 16
VMEM_BYTES_V7X = 64 * 1024 * 1024

W_QKV = 3 * H_A * DK_A
W_Z = H_A * DV_A
W_BG = LANES
W_QB = HQ_B * DH_B
W_KV = 2 * HKV_B * DH_B
W_GATE = 2 * D_MODEL
OFF_Z = W_QKV
OFF_BG = OFF_Z + W_Z
OFF_QB = OFF_BG + W_BG
OFF_KV = OFF_QB + W_QB
OFF_GATE = OFF_KV + W_KV
W_TOTAL = OFF_GATE + W_GATE
W_ROPE = W_QB + HKV_B * DH_B

NEG_BIG = -1e30
HEAD_PERM = (0, 4, 1, 5, 2, 6, 3, 7)

SUPER = 4 * CHUNK
HALF = CHUNK // 2
MOE_CHUNK = 192
MOE_CHUNK_SHORT = 128


def _vmem_limit(nbytes):
    return int(min(max(nbytes, 16 * 1024 * 1024), VMEM_BYTES_V7X - 8 * 1024 * 1024))


def _split3(x):
    hi = x.astype(bf16)
    r1 = x - hi.astype(f32)
    mid = r1.astype(bf16)
    lo = (r1 - mid.astype(f32)).astype(bf16)
    return hi, mid, lo


def _dot(a, b):
    return jnp.dot(a, b, preferred_element_type=f32)


def _dot_nt(a, b):
    return lax.dot_general(a, b, (((1,), (1,)), ((), ())), preferred_element_type=f32)


def _dot_exact_lhs_mask(mask_bf, parts):
    return _dot(mask_bf, parts[0]) + _dot(mask_bf, parts[1]) + _dot(mask_bf, parts[2])


def _pack_halves(x):
    w = x.shape[1] // 2
    lo = lax.bitcast_convert_type(x[:, 0:w].astype(bf16).astype(f32), jnp.uint32)
    hi = lax.bitcast_convert_type(x[:, w:2 * w].astype(bf16).astype(f32), jnp.uint32)
    return lax.shift_right_logical(lo, jnp.uint32(16)) | (hi & jnp.uint32(0xFFFF0000))


def _unpack_halves(p):
    lo = lax.bitcast_convert_type(lax.shift_left(p, jnp.uint32(16)), f32)
    hi = lax.bitcast_convert_type(p & jnp.uint32(0xFFFF0000), f32)
    return lo, hi


def _sigmoid(x):
    return 1.0 / (1.0 + jnp.exp(-x))


def _silu(x):
    return x * _sigmoid(x)


def _softplus(x):
    return jnp.maximum(x, 0.0) + jnp.log1p(jnp.exp(-jnp.abs(x)))


def _adaln_kernel(c_ref, w_ref, b_ref, o_ref):
    c = c_ref[...]
    s = _silu(c).astype(bf16)
    o_ref[...] = _dot(s, w_ref[...].astype(bf16)) + b_ref[...]


def adaln_all(cond, w_mod, b_mod):
    depth = w_mod.shape[0]
    r = cond.shape[0]
    nblk = 6
    return pl.pallas_call(
        _adaln_kernel,
        out_shape=jax.ShapeDtypeStruct((depth, r, 6 * D_MODEL), f32),
        grid=(depth, nblk),
        in_specs=[
            pl.BlockSpec((r, D_MODEL), lambda l, j: (0, 0)),
            pl.BlockSpec((None, D_MODEL, D_MODEL), lambda l, j: (l, 0, j)),
            pl.BlockSpec((None, 1, D_MODEL), lambda l, j: (l, 0, j)),
        ],
        out_specs=pl.BlockSpec((None, r, D_MODEL), lambda l, j: (l, 0, j)),
        compiler_params=pltpu.CompilerParams(
            dimension_semantics=("parallel", "parallel"),
            vmem_limit_bytes=_vmem_limit(24 * 1024 * 1024)),
        name="adaln",
    )(cond, w_mod, b_mod.reshape(depth, 1, 6 * D_MODEL))


def _inproj_kernel(x_ref, g_ref, sh_ref, sc_ref, cos_ref, sa_ref, sb_ref, alog_ref, dtb_ref, w_ref,
                   qkv_ref, z_ref, bg_ref, qb_ref, kv_ref, gate_ref):
    x = x_ref[...]
    ms = jnp.mean(x * x, axis=-1, keepdims=True)
    y = x * lax.rsqrt(ms + EPS) * g_ref[...]
    h = (y * (1.0 + sc_ref[...]) + sh_ref[...]).astype(bf16)

    qkv_ref[...] = _dot(h, w_ref[:, 0:W_QKV]).astype(bf16)
    z_ref[...] = _dot(h, w_ref[:, OFF_Z:OFF_Z + W_Z]).astype(bf16)
    raw = _dot(h, w_ref[:, OFF_BG:OFF_BG + W_BG])
    lane = lax.broadcasted_iota(jnp.int32, raw.shape, 1)
    bg_ref[...] = jnp.where(lane < 2 * H_A, _sigmoid(raw),
                            -jnp.exp(alog_ref[...]) * _softplus(raw + dtb_ref[...]))

    pr = _dot(h, w_ref[:, OFF_QB:OFF_QB + W_ROPE])
    reps = W_ROPE // LANES
    cos = jnp.concatenate([cos_ref[...]] * reps, axis=1)
    sa = jnp.concatenate([sa_ref[...]] * reps, axis=1)
    sb = jnp.concatenate([sb_ref[...]] * reps, axis=1)
    half = DH_B // 4
    rot = pr * cos + pltpu.roll(pr, W_ROPE - half, 1) * sa + pltpu.roll(pr, half, 1) * sb
    qb_ref[...] = rot[:, 0:W_QB].astype(bf16)
    kv_ref[:, 0:LANES] = rot[:, W_QB:W_ROPE].astype(bf16)
    kv_ref[:, LANES:2 * LANES] = _dot(h, w_ref[:, OFF_KV + LANES:OFF_KV + 2 * LANES]).astype(bf16)

    gate_ref[...] = _sigmoid(_dot(h, w_ref[:, OFF_GATE:OFF_GATE + W_GATE])).astype(bf16)


def inproj(x_all, mod_l, norm_g, w_cat, rope_tabs, a_log_l, dt_bias_l, *, n_lat_tiles_per_seq, n_lat_tiles, batch,
           tm):
    t_all = x_all.shape[0]
    n_tiles = t_all // tm
    cos_t, sa_t, sb_t = rope_tabs
    pad_lanes = lambda v: jnp.zeros((1, LANES), f32).at[0, 2 * H_A:4 * H_A].set(v.reshape(-1))
    alog_v = pad_lanes(a_log_l)
    dtb_v = pad_lanes(dt_bias_l)

    def seg(i):
        return jnp.where(i < n_lat_tiles, i // n_lat_tiles_per_seq, batch)

    def pos(i):
        return jnp.where(i < n_lat_tiles, i % n_lat_tiles_per_seq, n_lat_tiles_per_seq)

    row = lambda w: pl.BlockSpec((tm, w), lambda i: (i, 0))
    tab = pl.BlockSpec((tm, LANES), lambda i: (pos(i), 0))
    out_shapes = (
        jax.ShapeDtypeStruct((t_all, W_QKV), bf16),
        jax.ShapeDtypeStruct((t_all, W_Z), bf16),
        jax.ShapeDtypeStruct((t_all, W_BG), f32),
        jax.ShapeDtypeStruct((t_all, W_QB), bf16),
        jax.ShapeDtypeStruct((t_all, W_KV), bf16),
        jax.ShapeDtypeStruct((t_all, W_GATE), bf16),
    )
    vmem = 2 * (W_TOTAL * D_MODEL * 2) + 2 * tm * (D_MODEL * 4 + W_TOTAL * 2 + 3 * LANES * 4) + tm * W_TOTAL * 4
    return pl.pallas_call(
        _inproj_kernel,
        out_shape=out_shapes,
        grid=(n_tiles,),
        in_specs=[
            row(D_MODEL),
            pl.BlockSpec((1, D_MODEL), lambda i: (0, 0)),
            pl.BlockSpec((None, 1, D_MODEL), lambda i: (seg(i), 0, 0)),
            pl.BlockSpec((None, 1, D_MODEL), lambda i: (seg(i), 0, 1)),
            tab, tab, tab,
            pl.BlockSpec((1, LANES), lambda i: (0, 0)),
            pl.BlockSpec((1, LANES), lambda i: (0, 0)),
            pl.BlockSpec((D_MODEL, W_TOTAL), lambda i: (0, 0)),
        ],
        out_specs=(row(W_QKV), row(W_Z), row(W_BG), row(W_QB), row(W_KV), row(W_GATE)),
        compiler_params=pltpu.CompilerParams(
            dimension_semantics=("parallel",), vmem_limit_bytes=_vmem_limit(vmem)),
        name="inproj",
    )(x_all, norm_g.reshape(1, D_MODEL), mod_l, mod_l, cos_t, sa_t, sb_t, alog_v, dtb_v, w_cat)


def _gdn_kernel(q_ref, k_ref, v_ref, bg_ref, z_ref, cwq_ref, cwk_ref, cwv_ref, gn_ref, s0f_ref, s0b_ref,
                yin_ref, y_ref, sf_ref, sb_ref,
                u_s, w_s, qg_s, ak_s, gl_s, o_s, st_s, a2_s, x_s, *, seq, nsuper):
    del yin_ref
    hd = pl.program_id(1)
    nch = seq // CHUNK
    cps = SUPER // CHUNK

    ii = lax.broadcasted_iota(jnp.int32, (SUPER, SUPER), 0)
    jj = lax.broadcasted_iota(jnp.int32, (SUPER, SUPER), 1)
    same = (ii // CHUNK) == (jj // CHUNK)
    tri = (same & (jj <= ii), same & (jj >= ii))
    tri_s = (same & (jj < ii), same & (jj > ii))
    cum2_bf = jnp.concatenate([jnp.where(m, 1.0, 0.0) for m in tri], axis=0).astype(bf16)
    lane = lax.broadcasted_iota(jnp.int32, (SUPER, LANES), 1)

    def conv_silu(ref, cw_ref, base, st):
        main = ref[pl.ds(base, SUPER), :].astype(f32)
        p0 = pl.multiple_of(jnp.maximum(base - BF16_ROWS, 0), BF16_ROWS)
        n0 = pl.multiple_of(jnp.minimum(base + SUPER, seq - BF16_ROWS), BF16_ROWS)
        prev = ref[pl.ds(p0, BF16_ROWS), :].astype(f32) * jnp.where(st > 0, 1.0, 0.0)
        nxt = ref[pl.ds(n0, BF16_ROWS), :].astype(f32) * jnp.where(st < nsuper - 1, 1.0, 0.0)
        win = jnp.concatenate([prev, main, nxt], axis=0)
        acc = jnp.zeros((SUPER, LANES), f32)
        for i in range(CONV_K):
            lo_row = BF16_ROWS - CONV_K // 2 + i
            acc = acc + win[lo_row:lo_row + SUPER, :] * cw_ref[i:i + 1, :]
        return _silu(acc)

    def column(x, c):
        col = jnp.sum(jnp.where(lane == c, x, 0.0), axis=-1, keepdims=True)
        return jnp.broadcast_to(col, (SUPER, LANES))

    def front(st):
        base = pl.multiple_of(st * SUPER, SUPER)
        q = conv_silu(q_ref, cwq_ref, base, st)
        yield
        k = conv_silu(k_ref, cwk_ref, base, st)
        yield
        v = conv_silu(v_ref, cwv_ref, base, st)
        yield
        q = q * lax.rsqrt(jnp.sum(q * q, axis=-1, keepdims=True) + EPS) * (DK_A ** -0.5)
        k = k * lax.rsqrt(jnp.sum(k * k, axis=-1, keepdims=True) + EPS)
        q_bf = q.astype(bf16)
        k_bf = k.astype(bf16)
        qkk = _dot_nt(jnp.concatenate([q_bf, k_bf], axis=0), k_bf)
        qk_raw = qkk[0:SUPER]
        kk_raw = qkk[SUPER:2 * SUPER]
        bg = bg_ref[pl.ds(base, SUPER), :]
        rows = pl.ds(base, SUPER)
        yield

        g_both = [column(bg, 2 * H_A + d * H_A + hd) for d in range(2)]
        parts = [jnp.concatenate(p, axis=1) for p in zip(_split3(g_both[0]), _split3(g_both[1]))]
        csum_both = _dot_exact_lhs_mask(cum2_bf, parts)
        yield

        for d in range(2):
            beta = column(bg, d * H_A + hd)
            mask, mask_s = tri[d], tri_s[d]
            csum = csum_both[d * SUPER:(d + 1) * SUPER, d * LANES:(d + 1) * LANES]
            ct = jnp.concatenate([csum[0:LANES].T, csum[LANES:2 * LANES].T], axis=1)
            ct = jnp.concatenate([ct, ct], axis=0)
            c2 = jnp.concatenate([csum, csum], axis=1)
            decay = jnp.where(mask, jnp.exp(jnp.where(mask, c2 - ct, 0.0)), 0.0)
            yield
            beta2 = jnp.concatenate([beta, beta], axis=1)
            a_mat = jnp.where(mask_s, beta2 * kk_raw * decay, 0.0)
            if d == 1:
                a_mat = a_mat.T
            for c in range(cps):
                cs = slice(c * CHUNK, (c + 1) * CHUNK)
                blk = a_mat[cs, cs]
                prob = d * nch + st * cps + c
                a2_s[pl.ds(pl.multiple_of(prob * HALF, HALF), HALF), :] = jnp.concatenate(
                    [blk[0:HALF], blk[HALF:CHUNK]], axis=1)
            attn = jnp.where(mask, qk_raw * decay, 0.0)
            yield

            last = (CHUNK - 1, 0)[d]
            tot = jnp.concatenate(
                [jnp.broadcast_to(csum[c * CHUNK + last:c * CHUNK + last + 1, :], (CHUNK, LANES))
                 for c in range(cps)], axis=0)
            egc = jnp.exp(csum)
            u_s[d, rows, :] = (v * beta).astype(bf16)
            w_s[d, rows, :] = (k * beta * egc).astype(bf16)
            qg_s[d, rows, :] = (q * egc).astype(bf16)
            yield
            kg_t = (k * jnp.exp(tot - csum)).T
            glast = jnp.exp(tot)
            for c in range(cps):
                cs = slice(c * CHUNK, (c + 1) * CHUNK)
                ak_s[d, st * cps + c] = jnp.concatenate([attn[cs, cs], kg_t[:, cs]], axis=0).astype(bf16)
                gl_s[d, pl.ds(st * cps + c, 1), :] = glast[c * CHUNK:c * CHUNK + 1, :]
            yield

    def back(st):
        rows = pl.ds(pl.multiple_of(st * SUPER, SUPER), SUPER)
        lane2 = lax.broadcasted_iota(jnp.int32, (CHUNK, SUPER), 1)
        for d in range(2):
            blocks = []
            for c in range(cps):
                prob = d * nch + st * cps + c
                half = a2_s[pl.ds(pl.multiple_of(prob * HALF, HALF), HALF), :]
                blk = jnp.concatenate([half[:, 0:CHUNK], half[:, CHUNK:2 * CHUNK]], axis=0)
                wide = jnp.concatenate([blk] * cps, axis=1)
                blocks.append(jnp.where(lane2 // CHUNK == c, wide, 0.0))
            t_m = jnp.concatenate(blocks, axis=0)
            if d == 1:
                t_m = t_m.T
            yield
            rhs = jnp.concatenate([u_s[d, rows, :], w_s[d, rows, :]], axis=1)
            sol = rhs.astype(f32) + _dot(t_m.astype(bf16), rhs)
            u_s[d, rows, :] = sol[:, 0:DV_A].astype(bf16)
            w_s[d, rows, :] = sol[:, DV_A:2 * DV_A].astype(bf16)
            yield

    def interleave(*gens):
        active = list(gens)
        while active:
            for g in list(active):
                if next(g, StopIteration) is StopIteration:
                    active.remove(g)

    nprob = 2 * nch
    if nprob < LANES:
        a2_s[...] = jnp.zeros_like(a2_s)

    tiles_per_body = 2 if nsuper % 2 == 0 else 1

    def front_loop(m, carry):
        interleave(*[front(m * tiles_per_body + t) for t in range(tiles_per_body)])
        return carry

    lax.fori_loop(0, nsuper // tiles_per_body, front_loop, 0)

    for r in range(HALF):
        xt = a2_s[pl.ds(r, LANES, stride=HALF), :].T
        x_s[r] = xt[0:CHUNK]
        x_s[r + HALF] = xt[CHUNK:2 * CHUNK]

    nblk = CHUNK // SUBLANES
    for i in range(1, CHUNK):
        acc = [-x_s[i, jb * SUBLANES:(jb + 1) * SUBLANES, :] for jb in range(nblk)]
        for k in range(1, i):
            a_ik = x_s[i, k:k + 1, :]
            for jb in range((k - 1) // SUBLANES + 1):
                acc[jb] = acc[jb] - a_ik * x_s[k, jb * SUBLANES:(jb + 1) * SUBLANES, :]
        for jb in range(nblk):
            x_s[i, jb * SUBLANES:(jb + 1) * SUBLANES, :] = acc[jb]

    for r in range(HALF):
        yt = jnp.concatenate([x_s[r], x_s[r + HALF]], axis=0).T
        a2_s[pl.ds(r, LANES, stride=HALF), :] = yt

    def back_loop(st, carry):
        interleave(back(st))
        return carry

    lax.fori_loop(0, nsuper, back_loop, 0)

    st_s[0] = s0f_ref[...]
    st_s[1] = s0b_ref[...]

    def scan_step(n, accumulate):
        for d in range(2):
            c = n if d == 0 else nch - 1 - n
            rows = pl.ds(pl.multiple_of(c * CHUNK, CHUNK), CHUNK)
            s_old = st_s[d]
            ws = _dot(jnp.concatenate([w_s[d, rows, :], qg_s[d, rows, :]], axis=0), s_old.astype(bf16))
            v_new = (u_s[d, rows, :].astype(f32) - ws[0:CHUNK]).astype(bf16)
            ov = _dot(ak_s[d, c], v_new)
            o = ws[CHUNK:2 * CHUNK] + ov[0:CHUNK]
            st_s[d] = s_old * gl_s[d, pl.ds(c, 1), :] + ov[CHUNK:CHUNK + DK_A]
            if accumulate:
                o_s[rows, :] = o_s[rows, :] + o
            else:
                o_s[rows, :] = o

    def scan_a(n, carry):
        scan_step(n, False)
        return carry

    def scan_b(n, carry):
        scan_step(n, True)
        return carry

    lax.fori_loop(0, nch // 2, scan_a, 0)
    lax.fori_loop(nch // 2, nch, scan_b, 0)

    def finish(st, carry):
        rows = pl.ds(pl.multiple_of(st * SUPER, SUPER), SUPER)
        o = o_s[rows, :]
        yn = o * lax.rsqrt(jnp.mean(o * o, axis=-1, keepdims=True) + EPS) * gn_ref[...]
        y_ref[rows, :] = (yn * _silu(z_ref[rows, :].astype(f32))).astype(bf16)
        return carry

    lax.fori_loop(0, nsuper, finish, 0)
    sf_ref[...] = st_s[0]
    sb_ref[...] = st_s[1]


def gdn(qkv_all, bg_all, z_all, ya_all, conv_w, gdn_g, s0f, s0b, *, batch, seq, row_block0):
    nsuper = seq // SUPER
    nch = seq // CHUNK
    nch_pad = max(nch, SUBLANES)
    cw = jnp.zeros((3 * H_A, SUBLANES, LANES), f32).at[:, :CONV_K, :].set(
        conv_w.reshape(CONV_K, 3 * H_A, LANES).transpose(1, 0, 2))

    col = lambda off: pl.BlockSpec((seq, LANES), lambda b, h: (row_block0 + b, off + h))
    cwspec = lambda off: pl.BlockSpec((None, SUBLANES, LANES), lambda b, h: (off + h, 0, 0))
    st_spec = pl.BlockSpec((None, None, DK_A, DV_A), lambda b, h: (b, h, 0, 0))
    kern = functools.partial(_gdn_kernel, seq=seq, nsuper=nsuper)
    vmem = (2 * seq * LANES * (3 * 2 + 2 + 4 + 2) + seq * LANES * (6 * 2 + 6 * 2 + 4)
            + 4 * 4 * DK_A * DV_A * 4 + 20 * 1024 * 1024)
    return pl.pallas_call(
        kern,
        out_shape=(
            jax.ShapeDtypeStruct(ya_all.shape, bf16),
            jax.ShapeDtypeStruct((batch, H_A, DK_A, DV_A), f32),
            jax.ShapeDtypeStruct((batch, H_A, DK_A, DV_A), f32),
        ),
        grid=(batch, H_A),
        in_specs=[
            col(0), col(H_A), col(2 * H_A),
            pl.BlockSpec((seq, LANES), lambda b, h: (row_block0 + b, 0)),
            col(0),
            cwspec(0), cwspec(H_A), cwspec(2 * H_A),
            pl.BlockSpec((1, DV_A), lambda b, h: (0, 0)),
            st_spec, st_spec,
            pl.BlockSpec(memory_space=pl.ANY),
        ],
        out_specs=(col(0), st_spec, st_spec),
        scratch_shapes=[
            pltpu.VMEM((2, seq, LANES), bf16),
            pltpu.VMEM((2, seq, LANES), bf16),
            pltpu.VMEM((2, seq, LANES), bf16),
            pltpu.VMEM((2, nch, CHUNK + DK_A, CHUNK), bf16),
            pltpu.VMEM((2, nch_pad, LANES), f32),
            pltpu.VMEM((seq, LANES), f32),
            pltpu.VMEM((2, DK_A, DV_A), f32),
            pltpu.VMEM((LANES * HALF, LANES), f32),
            pltpu.VMEM((CHUNK, CHUNK, LANES), f32),
        ],
        input_output_aliases={11: 0},
        compiler_params=pltpu.CompilerParams(
            dimension_semantics=("parallel", "parallel"), vmem_limit_bytes=_vmem_limit(vmem)),
        name="gdn",
    )(qkv_all, qkv_all, qkv_all, bg_all, z_all, cw, cw, cw, gdn_g.reshape(1, DV_A), s0f, s0b, ya_all)


def _attn_kernel(sink_ref, q_ref, kp_ref, kc_ref, kn_ref, kx_ref, o_ref, *, n_lat_blocks, blocks_per_seq):
    i = pl.program_id(0)
    is_lat = i < n_lat_blocks
    n = i % blocks_per_seq
    kv = jnp.concatenate([kp_ref[...], kc_ref[...], kn_ref[...], kx_ref[...]], axis=0)
    k2 = kv[:, 0:LANES]
    v2 = kv[:, LANES:2 * LANES]
    nctx = kx_ref.shape[0]

    row = lax.broadcasted_iota(jnp.int32, (BLK, BLK), 0)
    col = lax.broadcasted_iota(jnp.int32, (BLK, BLK), 1)
    ok_prev = (col >= row) & is_lat & (n > 0)
    ok_cent = jnp.broadcast_to(is_lat, (BLK, BLK))
    ok_next = (col <= row) & is_lat & (n < blocks_per_seq - 1)
    zeros = jnp.zeros((BLK, BLK), f32)
    bias = jnp.concatenate(
        [jnp.where(ok_prev, zeros, NEG_BIG), jnp.where(ok_cent, zeros, NEG_BIG),
         jnp.where(ok_next, zeros, NEG_BIG), jnp.zeros((BLK, nctx), f32)], axis=1)
    bias2 = jnp.concatenate([bias, bias], axis=0)

    lane = lax.broadcasted_iota(jnp.int32, (BLK, LANES), 1)
    lo = lane < DH_B
    lo_bf = jnp.where(lo, 1.0, 0.0).astype(bf16)
    hi_bf = jnp.where(lo, 0.0, 1.0).astype(bf16)
    rows2 = lax.broadcasted_iota(jnp.int32, (2 * BLK, 1), 0)
    for p in range(HQ_B // 2):
        q2 = q_ref[:, p * LANES:(p + 1) * LANES]
        qs = jnp.concatenate([q2 * lo_bf, q2 * hi_bf], axis=0)
        s = _dot_nt(qs, k2) + bias2
        sink = jnp.where(rows2 < BLK, sink_ref[p], sink_ref[HQ_B // 2 + p])
        m = jnp.maximum(jnp.max(s, axis=-1, keepdims=True), sink)
        e = jnp.exp(s - m)
        den = jnp.sum(e, axis=-1, keepdims=True) + jnp.exp(sink - m)
        o = _dot(e.astype(bf16), v2) * (1.0 / den)
        o_ref[:, p * LANES:(p + 1) * LANES] = jnp.where(lo, o[0:BLK], o[BLK:2 * BLK]).astype(bf16)


def attention(qb_all, kv_all, sink, *, batch, seq, ctx_len, with_ctx_queries):
    t_lat = batch * seq
    bps = seq // BLK
    n_lat = t_lat // BLK
    cps = ctx_len // BLK
    n_blocks = n_lat + (batch * cps if with_ctx_queries else 0)
    ctx_blk0 = t_lat // ctx_len

    def bidx(i):
        return jnp.where(i < n_lat, i // bps, (i - n_lat) // cps)

    def nbr(delta):
        def f(i):
            n = i % bps
            j = jnp.clip(n + delta, 0, bps - 1)
            return jnp.where(i < n_lat, (i // bps) * bps + j, 0)
        return f

    kern = functools.partial(_attn_kernel, n_lat_blocks=n_lat, blocks_per_seq=bps)
    return pl.pallas_call(
        kern,
        out_shape=jax.ShapeDtypeStruct((n_blocks * BLK, W_QB), bf16),
        grid=(n_blocks,),
        in_specs=[
            pl.BlockSpec(memory_space=pltpu.SMEM),
            pl.BlockSpec((BLK, W_QB), lambda i: (i, 0)),
            pl.BlockSpec((BLK, W_KV), lambda i: (nbr(-1)(i), 0)),
            pl.BlockSpec((BLK, W_KV), lambda i: (nbr(0)(i), 0)),
            pl.BlockSpec((BLK, W_KV), lambda i: (nbr(1)(i), 0)),
            pl.BlockSpec((ctx_len, W_KV), lambda i: (ctx_blk0 + bidx(i), 0)),
        ],
        out_specs=pl.BlockSpec((BLK, W_QB), lambda i: (i, 0)),
        compiler_params=pltpu.CompilerParams(
            dimension_semantics=("parallel",), vmem_limit_bytes=_vmem_limit(32 * 1024 * 1024)),
        name="attn",
    )(sink, qb_all, kv_all, kv_all, kv_all, kv_all)


def _merge_kernel(x_ref, ya_ref, yb_ref, gate_ref, woa_ref, wob_ref, wo_ref, g1_ref, n2_ref, sh2_ref, sc2_ref,
                  wrh_ref, wrl_ref, rb_ref, x1_ref, hp_ref, e12_ref, wk_ref):
    pa = _dot(ya_ref[...], woa_ref[...])
    pb = _dot(yb_ref[...], wob_ref[...])
    m = gate_ref[:, 0:D_MODEL].astype(f32) * pa + gate_ref[:, D_MODEL:2 * D_MODEL].astype(f32) * pb
    y = _dot(m.astype(bf16), wo_ref[...])
    x1 = x_ref[...] + g1_ref[...] * y
    x1_ref[...] = x1
    ms = jnp.mean(x1 * x1, axis=-1, keepdims=True)
    h2 = (x1 * lax.rsqrt(ms + EPS) * n2_ref[...]) * (1.0 + sc2_ref[...]) + sh2_ref[...]
    h_hi = h2.astype(bf16)
    hp_ref[...] = _pack_halves(h_hi.astype(f32))
    h_lo = (h2 - h_hi.astype(f32)).astype(bf16)

    logit = _dot_nt(wrh_ref[...], h_hi) + _dot_nt(wrl_ref[...], h_hi) + _dot_nt(wrh_ref[...], h_lo)
    score = _sigmoid(logit)
    sel = score + rb_ref[...]
    tm = sel.shape[1]

    def top2_in_group(g):
        r = [sel[g * EXP_PER_GROUP + j:g * EXP_PER_GROUP + j + 1, :] for j in range(EXP_PER_GROUP)]
        pair = None
        for a in range(EXP_PER_GROUP):
            for b in range(a + 1, EXP_PER_GROUP):
                s_ab = r[a] + r[b]
                pair = s_ab if pair is None else jnp.maximum(pair, s_ab)
        best = r[0]
        i1 = jnp.zeros((1, tm), jnp.int32)
        for j in range(1, EXP_PER_GROUP):
            better = r[j] > best
            best = jnp.where(better, r[j], best)
            i1 = jnp.where(better, j, i1)
        second = jnp.full((1, tm), -jnp.inf, f32)
        i2 = jnp.zeros((1, tm), jnp.int32)
        for j in range(EXP_PER_GROUP):
            cand = jnp.where(i1 == j, -jnp.inf, r[j])
            better = cand > second
            second = jnp.where(better, cand, second)
            i2 = jnp.where(better, j, i2)
        return pair, i1 + g * EXP_PER_GROUP, i2 + g * EXP_PER_GROUP

    gs, e1, e2 = top2_in_group(0)
    for g in range(1, N_GROUPS):
        gs_g, e1_g, e2_g = top2_in_group(g)
        better = gs_g > gs
        gs = jnp.where(better, gs_g, gs)
        e1 = jnp.where(better, e1_g, e1)
        e2 = jnp.where(better, e2_g, e2)

    eidx = lax.broadcasted_iota(jnp.int32, (N_EXPERTS, tm), 0)
    hit1 = eidx == e1
    hit2 = eidx == e2
    s1 = jnp.sum(jnp.where(hit1, score, 0.0), axis=0, keepdims=True)
    s2 = jnp.sum(jnp.where(hit2, score, 0.0), axis=0, keepdims=True)
    tot = s1 + s2
    e12_ref[...] = jnp.concatenate([e1, e2], axis=0)
    wk_pad = jnp.concatenate([s1 / tot, s2 / tot, jnp.zeros((LANES - 2, tm), f32)], axis=0)
    wk_ref[...] = wk_pad.T


def merge(x_all, ya_all, yb_all, gate_all, w_oa, w_ob, w_o, mod_l, norm2_g, wr_hi, wr_lo, router_bias,
          *, n_rows, n_lat_tiles_per_seq, n_lat_tiles, batch, tm):
    n_tiles = n_rows // tm

    def seg(i):
        return jnp.where(i < n_lat_tiles, i // n_lat_tiles_per_seq, batch)

    row = lambda w: pl.BlockSpec((tm, w), lambda i: (i, 0))
    full = lambda a, b: pl.BlockSpec((a, b), lambda i: (0, 0))
    modspec = lambda j: pl.BlockSpec((None, 1, D_MODEL), lambda i: (seg(i), 0, j))
    vmem = 2 * (2 * W_Z * D_MODEL * 2 + D_MODEL * D_MODEL * 2) + 2 * tm * (
        D_MODEL * 4 * 2 + 2 * W_Z * 2 + W_GATE * 2 + D_MODEL * 2 + LANES * 4) + 8 * tm * D_MODEL * 4
    return pl.pallas_call(
        _merge_kernel,
        out_shape=(
            jax.ShapeDtypeStruct((n_rows, D_MODEL), f32),
            jax.ShapeDtypeStruct((n_rows, D_MODEL // 2), jnp.uint32),
            jax.ShapeDtypeStruct((2, n_rows), jnp.int32),
            jax.ShapeDtypeStruct((n_rows, LANES), f32),
        ),
        grid=(n_tiles,),
        in_specs=[
            row(D_MODEL), row(W_Z), row(W_QB), row(W_GATE),
            full(W_Z, D_MODEL), full(W_QB, D_MODEL), full(D_MODEL, D_MODEL),
            modspec(2), full(1, D_MODEL), modspec(3), modspec(4),
            full(N_EXPERTS, D_MODEL), full(N_EXPERTS, D_MODEL), full(N_EXPERTS, 1),
        ],
        out_specs=(row(D_MODEL), row(D_MODEL // 2), pl.BlockSpec((2, tm), lambda i: (0, i)), row(LANES)),
        compiler_params=pltpu.CompilerParams(
            dimension_semantics=("parallel",), vmem_limit_bytes=_vmem_limit(vmem)),
        name="merge",
    )(x_all, ya_all, yb_all, gate_all, w_oa, w_ob, w_o, mod_l, norm2_g.reshape(1, D_MODEL), mod_l, mod_l,
      wr_hi, wr_lo, router_bias.reshape(N_EXPERTS, 1))


def _moe_kernel(ord_ref, tok_ref, st_ref, x1_ref, hp_ref, wk_ref, wg_ref, wu_ref, wd_ref, g2_ref, fg_ref, o_ref,
                slot_s, xg_s, yp_s, *, tm, final_norm):
    e = pl.program_id(1)
    half = D_MODEL // 2
    start = st_ref[0, e]
    count = st_ref[0, e + 1] - start

    @pl.when(e == 0)
    def _():
        xg_s[...] = jnp.zeros_like(xg_s)

    def chunk(c, carry):
        base = start + c * MOE_CHUNK
        ngroups = (jnp.minimum(count - c * MOE_CHUNK, MOE_CHUNK) + SUBLANES - 1) // SUBLANES

        def gather(g, carry2):
            r0 = pl.multiple_of(g * SUBLANES, SUBLANES)
            rows = [hp_ref[pl.ds(tok_ref[0, base + r0 + u], 1), :] for u in range(SUBLANES)]
            xg_s[pl.ds(r0, SUBLANES), :] = jnp.concatenate(rows, axis=0)
            return carry2

        lax.fori_loop(0, ngroups, gather, 0)

        def ffn(m):
            x_lo, x_hi = _unpack_halves(xg_s[0:m, :])
            x_lo = x_lo.astype(bf16)
            x_hi = x_hi.astype(bf16)
            a = _dot(x_lo, wg_ref[0:half, :]) + _dot(x_hi, wg_ref[half:D_MODEL, :])
            b = _dot(x_lo, wu_ref[0:half, :]) + _dot(x_hi, wu_ref[half:D_MODEL, :])
            he = (_silu(a) * b).astype(bf16)
            yp_s[0:m, :] = _pack_halves(_dot(he, wd_ref[...]))

        short = ngroups * SUBLANES <= MOE_CHUNK_SHORT

        @pl.when(short)
        def _():
            ffn(MOE_CHUNK_SHORT)

        @pl.when(jnp.logical_not(short))
        def _():
            ffn(MOE_CHUNK)

        def scatter(g, carry2):
            r0 = pl.multiple_of(g * SUBLANES, SUBLANES)
            blk = yp_s[pl.ds(r0, SUBLANES), :]
            for u in range(SUBLANES):
                slot_s[pl.ds(ord_ref[0, base + r0 + u], 1), :] = blk[u:u + 1, :]
            return carry2

        lax.fori_loop(0, ngroups, scatter, 0)
        return carry

    lax.fori_loop(0, (count + MOE_CHUNK - 1) // MOE_CHUNK, chunk, 0)

    @pl.when(e == N_EXPERTS - 1)
    def _():
        sub = 256
        lane = lax.broadcasted_iota(jnp.int32, (sub, LANES), 1)
        for r in range(tm // sub):
            rows = pl.ds(r * sub, sub)
            wk = wk_ref[rows, :]
            w1 = jnp.sum(jnp.where(lane == 0, wk, 0.0), axis=-1, keepdims=True)
            w2 = jnp.sum(jnp.where(lane == 1, wk, 0.0), axis=-1, keepdims=True)
            lo1, hi1 = _unpack_halves(slot_s[pl.ds(r * sub, sub), :])
            lo2, hi2 = _unpack_halves(slot_s[pl.ds(tm + r * sub, sub), :])
            acc = jnp.concatenate([w1 * lo1 + w2 * lo2, w1 * hi1 + w2 * hi2], axis=1)
            x2 = x1_ref[rows, :] + g2_ref[...] * acc
            if final_norm:
                ms = jnp.mean(x2 * x2, axis=-1, keepdims=True)
                x2 = x2 * lax.rsqrt(ms + EPS) * fg_ref[...]
            o_ref[rows, :] = x2


def moe(x1, hp, e12, wk, w_gate, w_up, w_down, mod_l, final_g, *, n_lat_tiles_per_seq, n_lat_tiles, batch, tm,
        final_norm):
    n_rows = x1.shape[0]
    n_tiles = n_rows // tm

    ids = e12.reshape(2, n_tiles, tm).transpose(1, 0, 2).reshape(n_tiles, 2 * tm)
    order = jnp.argsort(ids, axis=1, stable=True).astype(jnp.int32)
    tok = jnp.pad(jnp.where(order >= tm, order - tm, order), ((0, 0), (0, MOE_CHUNK)))
    order = jnp.pad(order, ((0, 0), (0, MOE_CHUNK)), constant_values=2 * tm)
    counts = jnp.sum((ids[:, :, None] == jnp.arange(N_EXPERTS, dtype=jnp.int32)[None, None, :]).astype(jnp.int32),
                     axis=1)
    starts = jnp.concatenate([jnp.zeros((n_tiles, 1), jnp.int32), jnp.cumsum(counts, axis=1)], axis=1)
    starts = jnp.pad(starts, ((0, 0), (0, 2 * N_EXPERTS - (N_EXPERTS + 1))))

    def seg(i):
        return jnp.where(i < n_lat_tiles, i // n_lat_tiles_per_seq, batch)

    row = lambda w: pl.BlockSpec((tm, w), lambda i, e: (i, 0))
    smem_row = lambda w: pl.BlockSpec((None, 1, w), lambda i, e: (i, 0, 0), memory_space=pltpu.SMEM)
    kern = functools.partial(_moe_kernel, tm=tm, final_norm=final_norm)
    vmem = (2 * 3 * D_MODEL * D_EXPERT * 2 + 2 * tm * (D_MODEL * 4 * 2 + D_MODEL * 2 + LANES * 4)
            + 2 * tm * D_MODEL * 2 + 16 * 1024 * 1024)
    return pl.pallas_call(
        kern,
        out_shape=jax.ShapeDtypeStruct((n_rows, D_MODEL), f32),
        grid=(n_tiles, N_EXPERTS),
        in_specs=[
            smem_row(2 * tm + MOE_CHUNK), smem_row(2 * tm + MOE_CHUNK), smem_row(2 * N_EXPERTS),
            row(D_MODEL), row(D_MODEL // 2), row(LANES),
            pl.BlockSpec((None, D_MODEL, D_EXPERT), lambda i, e: (e, 0, 0)),
            pl.BlockSpec((None, D_MODEL, D_EXPERT), lambda i, e: (e, 0, 0)),
            pl.BlockSpec((None, D_EXPERT, D_MODEL), lambda i, e: (e, 0, 0)),
            pl.BlockSpec((None, 1, D_MODEL), lambda i, e: (seg(i), 0, 5)),
            pl.BlockSpec((1, D_MODEL), lambda i, e: (0, 0)),
        ],
        out_specs=row(D_MODEL),
        scratch_shapes=[
            pltpu.VMEM((2 * tm + SUBLANES, D_MODEL // 2), jnp.uint32),
            pltpu.VMEM((MOE_CHUNK, D_MODEL // 2), jnp.uint32),
            pltpu.VMEM((MOE_CHUNK, D_MODEL // 2), jnp.uint32),
        ],
        compiler_params=pltpu.CompilerParams(
            dimension_semantics=("parallel", "arbitrary"), vmem_limit_bytes=_vmem_limit(vmem)),
        name="moe",
    )(order.reshape(n_tiles, 1, 2 * tm + MOE_CHUNK), tok.reshape(n_tiles, 1, 2 * tm + MOE_CHUNK),
      starts.reshape(n_tiles, 1, 2 * N_EXPERTS), x1, hp, wk,
      w_gate, w_up, w_down, mod_l, final_g.reshape(1, D_MODEL))


def _prep_w_in(w_in_l):
    qkv_a = 3 * H_A * DK_A
    o_z = qkv_a
    o_b = o_z + H_A * DV_A
    o_a = o_b + 2 * H_A
    o_q = o_a + 2 * H_A
    o_k = o_q + HQ_B * DH_B
    o_v = o_k + HKV_B * DH_B
    o_ga = o_v + HKV_B * DH_B
    w_bg = jnp.zeros((D_MODEL, W_BG), f32).at[:, :4 * H_A].set(w_in_l[:, o_b:o_q])
    wq = w_in_l[:, o_q:o_k].reshape(D_MODEL, HQ_B, DH_B)[:, np.array(HEAD_PERM), :].reshape(D_MODEL, W_QB)
    wq = wq * (DH_B ** -0.5)
    cat = jnp.concatenate(
        [w_in_l[:, :qkv_a], w_in_l[:, o_z:o_b], w_bg, wq, w_in_l[:, o_k:o_v], w_in_l[:, o_v:o_ga],
         w_in_l[:, o_ga:]], axis=1)
    return cat.astype(bf16)


def _rope_tables(seq, tm):
    t = np.arange(seq)
    rows = (t // GRID_W).astype(np.float32)
    cols = (t % GRID_W).astype(np.float32)
    nf = DH_B // 4
    inv = jnp.asarray(ROPE_THETA, f32) ** (-jnp.arange(nf, dtype=f32) / nf)
    ang_r = jnp.asarray(rows)[:, None] * inv[None, :]
    ang_c = jnp.asarray(cols)[:, None] * inv[None, :]
    cos64 = jnp.concatenate([jnp.cos(ang_r), jnp.cos(ang_r), jnp.cos(ang_c), jnp.cos(ang_c)], axis=1)
    zero = jnp.zeros_like(ang_r)
    sa64 = jnp.concatenate([-jnp.sin(ang_r), zero, -jnp.sin(ang_c), zero], axis=1)
    sb64 = jnp.concatenate([zero, jnp.sin(ang_r), zero, jnp.sin(ang_c)], axis=1)

    def fin(tab, fill):
        tab = jnp.concatenate([tab, tab], axis=1)
        return jnp.concatenate([tab, jnp.full((tm, LANES), fill, f32)], axis=0)

    return fin(cos64, 1.0), fin(sa64, 0.0), fin(sb64, 0.0)


TM_TOKEN = 256
TM_MERGE = 512
TM_MOE = 1024


def kernel(x, c, ctx, c_ctx, w_mod, b_mod, norm1_g, norm2_g, w_in, conv_w, a_log, dt_bias, gdn_norm_g, sink,
           w_oa, w_ob, w_o, w_router, router_bias, w_gate, w_up, w_down, final_g):
    batch, seq, d = x.shape
    ctx_len = ctx.shape[1]
    depth = w_mod.shape[0]
    assert d == D_MODEL and seq % SUPER == 0 and ctx_len % SUPER == 0 and seq % TM_MOE == 0
    t_lat = batch * seq
    t_ctx = batch * ctx_len
    assert t_ctx % TM_MOE == 0

    x_all = jnp.concatenate([x.reshape(t_lat, d), ctx.reshape(t_ctx, d)], axis=0)
    t_all = t_lat + t_ctx

    n_cond = ((batch + 1 + SUBLANES - 1) // SUBLANES) * SUBLANES
    cond = jnp.zeros((n_cond, d), f32).at[:batch].set(c).at[batch].set(c_ctx)
    mod = adaln_all(cond, w_mod, b_mod).reshape(depth, n_cond, 1, 6 * d)

    rope_tabs = _rope_tables(seq, TM_TOKEN)
    wr_hi = w_router.T.astype(bf16)
    wr_lo = (w_router.T - wr_hi.astype(f32)).astype(bf16)
    perm_rows = np.concatenate([np.arange(h * DH_B, (h + 1) * DH_B) for h in HEAD_PERM])
    zero_state = jnp.zeros((batch, H_A, DK_A, DV_A), f32)

    tok = dict(n_lat_tiles_per_seq=seq // TM_TOKEN, n_lat_tiles=t_lat // TM_TOKEN, batch=batch, tm=TM_TOKEN)
    tok_merge = dict(n_lat_tiles_per_seq=seq // TM_MERGE, n_lat_tiles=t_lat // TM_MERGE, batch=batch, tm=TM_MERGE)
    tok_moe = dict(n_lat_tiles_per_seq=seq // TM_MOE, n_lat_tiles=t_lat // TM_MOE, batch=batch, tm=TM_MOE)

    for l in range(depth):
        need_ctx = l < depth - 1
        w_cat = _prep_w_in(w_in[l])
        qkv_all, z_all, bg_all, qb_all, kv_all, gate_all = inproj(
            x_all, mod[l], norm1_g[l], w_cat, rope_tabs, a_log[l], dt_bias[l], **tok)

        ya_all = jnp.zeros((t_all, W_Z), bf16)
        ya_all, s_f, s_b = gdn(qkv_all, bg_all, z_all, ya_all, conv_w[l], gdn_norm_g[l], zero_state, zero_state,
                               batch=batch, seq=ctx_len, row_block0=t_lat // ctx_len)
        ya_all, _, _ = gdn(qkv_all, bg_all, z_all, ya_all, conv_w[l], gdn_norm_g[l], s_f, s_b,
                           batch=batch, seq=seq, row_block0=0)

        yb_all = attention(qb_all, kv_all, sink[l], batch=batch, seq=seq, ctx_len=ctx_len,
                           with_ctx_queries=need_ctx)

        n_rows = t_all if need_ctx else t_lat
        x1, hp, e12, wk = merge(
            x_all, ya_all, yb_all, gate_all, w_oa[l].astype(bf16), w_ob[l][perm_rows].astype(bf16),
            w_o[l].astype(bf16), mod[l], norm2_g[l], wr_hi, wr_lo, router_bias, n_rows=n_rows, **tok_merge)
        x_all = moe(x1, hp, e12, wk, w_gate[l].astype(bf16), w_up[l].astype(bf16), w_down[l].astype(bf16),
                    mod[l], final_g, final_norm=not need_ctx, **tok_moe)

    return x_all[:t_lat].reshape(batch, seq, d)
```

```python
import functools

import jax
import jax.numpy as jnp
import numpy as np
from jax import lax
from jax.experimental import pallas as pl
from jax.experimental.pallas import tpu as pltpu

f32 = jnp.float32
bf16 = jnp.bfloat16

D_MODEL = 1024
GRID_W = 64
EPS = 1e-6
H_A = 4
DK_A = 128
DV_A = 128
CONV_K = 5
CHUNK = 64
HQ_B = 8
HKV_B = 2
DH_B = 64
WINDOW = 128
BLK = 128
ROPE_THETA = 10000.0
N_EXPERTS = 16
N_GROUPS = 4
EXP_PER_GROUP = 4
D_EXPERT = 512

LANES = 128
SUBLANES = 8
BF16_ROWS = 16
VMEM_BYTES_V7X = 64 * 1024 * 1024

W_QKV = 3 * H_A * DK_A
W_Z = H_A * DV_A
W_BG = LANES
W_QB = HQ_B * DH_B
W_KV = 2 * HKV_B * DH_B
W_GATE = 2 * D_MODEL
OFF_Z = W_QKV
OFF_BG = OFF_Z + W_Z
OFF_QB = OFF_BG + W_BG
OFF_KV = OFF_QB + W_QB
OFF_GATE = OFF_KV + W_KV
W_TOTAL = OFF_GATE + W_GATE
W_ROPE = W_QB + HKV_B * DH_B

NEG_BIG = -1e30
HEAD_PERM = (0, 4, 1, 5, 2, 6, 3, 7)

SUPER = 4 * CHUNK
HALF = CHUNK // 2
MOE_CHUNK = 320
MOE_CHUNK_SHORT = 272


def _vmem_limit(nbytes):
    return int(min(max(nbytes, 16 * 1024 * 1024), VMEM_BYTES_V7X - 8 * 1024 * 1024))


def _split3(x):
    hi = x.astype(bf16)
    r1 = x - hi.astype(f32)
    mid = r1.astype(bf16)
    lo = (r1 - mid.astype(f32)).astype(bf16)
    return hi, mid, lo


def _dot(a, b):
    return jnp.dot(a, b, preferred_element_type=f32)


def _dot_nt(a, b):
    return lax.dot_general(a, b, (((1,), (1,)), ((), ())), preferred_element_type=f32)


def _dot_exact_lhs_mask(mask_bf, parts):
    return _dot(mask_bf, parts[0]) + _dot(mask_bf, parts[1]) + _dot(mask_bf, parts[2])


def _pack_halves(x):
    w = x.shape[1] // 2
    lo = lax.bitcast_convert_type(x[:, 0:w].astype(bf16).astype(f32), jnp.uint32)
    hi = lax.bitcast_convert_type(x[:, w:2 * w].astype(bf16).astype(f32), jnp.uint32)
    return lax.shift_right_logical(lo, jnp.uint32(16)) | (hi & jnp.uint32(0xFFFF0000))


def _unpack_halves(p):
    lo = lax.bitcast_convert_type(lax.shift_left(p, jnp.uint32(16)), f32)
    hi = lax.bitcast_convert_type(p & jnp.uint32(0xFFFF0000), f32)
    return lo, hi


def _sigmoid(x):
    return 1.0 / (1.0 + jnp.exp(-x))


def _silu(x):
    return x * _sigmoid(x)


def _softplus(x):
    return jnp.maximum(x, 0.0) + jnp.log1p(jnp.exp(-jnp.abs(x)))


def _adaln_kernel(c_ref, w_ref, b_ref, o_ref):
    c = c_ref[...]
    s = _silu(c).astype(bf16)
    o_ref[...] = _dot(s, w_ref[...].astype(bf16)) + b_ref[...]


def adaln_all(cond, w_mod, b_mod):
    depth = w_mod.shape[0]
    r = cond.shape[0]
    nblk = 6
    return pl.pallas_call(
        _adaln_kernel,
        out_shape=jax.ShapeDtypeStruct((depth, r, 6 * D_MODEL), f32),
        grid=(depth, nblk),
        in_specs=[
            pl.BlockSpec((r, D_MODEL), lambda l, j: (0, 0)),
            pl.BlockSpec((None, D_MODEL, D_MODEL), lambda l, j: (l, 0, j)),
            pl.BlockSpec((None, 1, D_MODEL), lambda l, j: (l, 0, j)),
        ],
        out_specs=pl.BlockSpec((None, r, D_MODEL), lambda l, j: (l, 0, j)),
        compiler_params=pltpu.CompilerParams(
            dimension_semantics=("parallel", "parallel"),
            vmem_limit_bytes=_vmem_limit(24 * 1024 * 1024)),
        name="adaln",
    )(cond, w_mod, b_mod.reshape(depth, 1, 6 * D_MODEL))


def _inproj_kernel(x_ref, g_ref, sh_ref, sc_ref, cos_ref, sa_ref, sb_ref, alog_ref, dtb_ref, w_ref,
                   qkv_ref, z_ref, bg_ref, qb_ref, kv_ref, gate_ref):
    x = x_ref[...]
    ms = jnp.mean(x * x, axis=-1, keepdims=True)
    y = x * lax.rsqrt(ms + EPS) * g_ref[...]
    h = (y * (1.0 + sc_ref[...]) + sh_ref[...]).astype(bf16)

    qkv_ref[...] = _dot(h, w_ref[:, 0:W_QKV]).astype(bf16)
    z_ref[...] = _dot(h, w_ref[:, OFF_Z:OFF_Z + W_Z]).astype(bf16)
    raw = _dot(h, w_ref[:, OFF_BG:OFF_BG + W_BG])
    lane = lax.broadcasted_iota(jnp.int32, raw.shape, 1)
    bg_ref[...] = jnp.where(lane < 2 * H_A, _sigmoid(raw),
                            -jnp.exp(alog_ref[...]) * _softplus(raw + dtb_ref[...]))

    pr = _dot(h, w_ref[:, OFF_QB:OFF_QB + W_ROPE])
    reps = W_ROPE // LANES
    cos = jnp.concatenate([cos_ref[...]] * reps, axis=1)
    sa = jnp.concatenate([sa_ref[...]] * reps, axis=1)
    sb = jnp.concatenate([sb_ref[...]] * reps, axis=1)
    half = DH_B // 4
    rot = pr * cos + pltpu.roll(pr, W_ROPE - half, 1) * sa + pltpu.roll(pr, half, 1) * sb
    qb_ref[...] = rot[:, 0:W_QB].astype(bf16)
    kv_ref[:, 0:LANES] = rot[:, W_QB:W_ROPE].astype(bf16)
    kv_ref[:, LANES:2 * LANES] = _dot(h, w_ref[:, OFF_KV + LANES:OFF_KV + 2 * LANES]).astype(bf16)

    gate_ref[...] = _sigmoid(_dot(h, w_ref[:, OFF_GATE:OFF_GATE + W_GATE])).astype(bf16)


def inproj(x_all, mod_l, norm_g, w_cat, rope_tabs, a_log_l, dt_bias_l, *, n_lat_tiles_per_seq, n_lat_tiles, batch,
           tm):
    t_all = x_all.shape[0]
    n_tiles = t_all // tm
    cos_t, sa_t, sb_t = rope_tabs
    pad_lanes = lambda v: jnp.zeros((1, LANES), f32).at[0, 2 * H_A:4 * H_A].set(v.reshape(-1))
    alog_v = pad_lanes(a_log_l)
    dtb_v = pad_lanes(dt_bias_l)

    def seg(i):
        return jnp.where(i < n_lat_tiles, i // n_lat_tiles_per_seq, batch)

    def pos(i):
        return jnp.where(i < n_lat_tiles, i % n_lat_tiles_per_seq, n_lat_tiles_per_seq)

    row = lambda w: pl.BlockSpec((tm, w), lambda i: (i, 0))
    tab = pl.BlockSpec((tm, LANES), lambda i: (pos(i), 0))
    out_shapes = (
        jax.ShapeDtypeStruct((t_all, W_QKV), bf16),
        jax.ShapeDtypeStruct((t_all, W_Z), bf16),
        jax.ShapeDtypeStruct((t_all, W_BG), f32),
        jax.ShapeDtypeStruct((t_all, W_QB), bf16),
        jax.ShapeDtypeStruct((t_all, W_KV), bf16),
        jax.ShapeDtypeStruct((t_all, W_GATE), bf16),
    )
    vmem = 2 * (W_TOTAL * D_MODEL * 2) + 2 * tm * (D_MODEL * 4 + W_TOTAL * 2 + 3 * LANES * 4) + tm * W_TOTAL * 4
    return pl.pallas_call(
        _inproj_kernel,
        out_shape=out_shapes,
        grid=(n_tiles,),
        in_specs=[
            row(D_MODEL),
            pl.BlockSpec((1, D_MODEL), lambda i: (0, 0)),
            pl.BlockSpec((None, 1, D_MODEL), lambda i: (seg(i), 0, 0)),
            pl.BlockSpec((None, 1, D_MODEL), lambda i: (seg(i), 0, 1)),
            tab, tab, tab,
            pl.BlockSpec((1, LANES), lambda i: (0, 0)),
            pl.BlockSpec((1, LANES), lambda i: (0, 0)),
            pl.BlockSpec((D_MODEL, W_TOTAL), lambda i: (0, 0)),
        ],
        out_specs=(row(W_QKV), row(W_Z), row(W_BG), row(W_QB), row(W_KV), row(W_GATE)),
        compiler_params=pltpu.CompilerParams(
            dimension_semantics=("parallel",), vmem_limit_bytes=_vmem_limit(vmem)),
        name="inproj",
    )(x_all, norm_g.reshape(1, D_MODEL), mod_l, mod_l, cos_t, sa_t, sb_t, alog_v, dtb_v, w_cat)


def _gdn_kernel(q_ref, k_ref, v_ref, bg_ref, z_ref, cwq_ref, cwk_ref, cwv_ref, gn_ref, s0f_ref, s0b_ref,
                yin_ref, y_ref, sf_ref, sb_ref,
                u_s, w_s, qg_s, ak_s, gl_s, o_s, st_s, a2_s, x_s, *, seq, nsuper):
    del yin_ref
    hd = pl.program_id(1)
    nch = seq // CHUNK
    cps = SUPER // CHUNK

    ii = lax.broadcasted_iota(jnp.int32, (SUPER, SUPER), 0)
    jj = lax.broadcasted_iota(jnp.int32, (SUPER, SUPER), 1)
    same = (ii // CHUNK) == (jj // CHUNK)
    tri = (same & (jj <= ii), same & (jj >= ii))
    tri_s = (same & (jj < ii), same & (jj > ii))
    cum2_bf = jnp.concatenate([jnp.where(m, 1.0, 0.0) for m in tri], axis=0).astype(bf16)
    lane = lax.broadcasted_iota(jnp.int32, (SUPER, LANES), 1)

    def conv_silu(ref, cw_ref, base, st):
        main = ref[pl.ds(base, SUPER), :].astype(f32)
        p0 = pl.multiple_of(jnp.maximum(base - BF16_ROWS, 0), BF16_ROWS)
        n0 = pl.multiple_of(jnp.minimum(base + SUPER, seq - BF16_ROWS), BF16_ROWS)
        prev = ref[pl.ds(p0, BF16_ROWS), :].astype(f32) * jnp.where(st > 0, 1.0, 0.0)
        nxt = ref[pl.ds(n0, BF16_ROWS), :].astype(f32) * jnp.where(st < nsuper - 1, 1.0, 0.0)
        win = jnp.concatenate([prev, main, nxt], axis=0)
        acc = jnp.zeros((SUPER, LANES), f32)
        for i in range(CONV_K):
            lo_row = BF16_ROWS - CONV_K // 2 + i
            acc = acc + win[lo_row:lo_row + SUPER, :] * cw_ref[i:i + 1, :]
        return _silu(acc)

    def column(x, c):
        col = jnp.sum(jnp.where(lane == c, x, 0.0), axis=-1, keepdims=True)
        return jnp.broadcast_to(col, (SUPER, LANES))

    def front(st):
        base = pl.multiple_of(st * SUPER, SUPER)
        q = conv_silu(q_ref, cwq_ref, base, st)
        yield
        k = conv_silu(k_ref, cwk_ref, base, st)
        yield
        v = conv_silu(v_ref, cwv_ref, base, st)
        yield
        q = q * lax.rsqrt(jnp.sum(q * q, axis=-1, keepdims=True) + EPS) * (DK_A ** -0.5)
        k = k * lax.rsqrt(jnp.sum(k * k, axis=-1, keepdims=True) + EPS)
        q_bf = q.astype(bf16)
        k_bf = k.astype(bf16)
        qkk = _dot_nt(jnp.concatenate([q_bf, k_bf], axis=0), k_bf)
        qk_raw = qkk[0:SUPER]
        kk_raw = qkk[SUPER:2 * SUPER]
        bg = bg_ref[pl.ds(base, SUPER), :]
        rows = pl.ds(base, SUPER)
        yield

        g_both = [column(bg, 2 * H_A + d * H_A + hd) for d in range(2)]
        parts = [jnp.concatenate(p, axis=1) for p in zip(_split3(g_both[0]), _split3(g_both[1]))]
        csum_both = _dot_exact_lhs_mask(cum2_bf, parts)
        yield

        for d in range(2):
            beta = column(bg, d * H_A + hd)
            mask, mask_s = tri[d], tri_s[d]
            csum = csum_both[d * SUPER:(d + 1) * SUPER, d * LANES:(d + 1) * LANES]
            ct = jnp.concatenate([csum[0:LANES].T, csum[LANES:2 * LANES].T], axis=1)
            ct = jnp.concatenate([ct, ct], axis=0)
            c2 = jnp.concatenate([csum, csum], axis=1)
            decay = jnp.where(mask, jnp.exp(jnp.where(mask, c2 - ct, 0.0)), 0.0)
            yield
            beta2 = jnp.concatenate([beta, beta], axis=1)
            a_mat = jnp.where(mask_s, beta2 * kk_raw * decay, 0.0)
            if d == 1:
                a_mat = a_mat.T
            for c in range(cps):
                cs = slice(c * CHUNK, (c + 1) * CHUNK)
                blk = a_mat[cs, cs]
                prob = d * nch + st * cps + c
                a2_s[pl.ds(pl.multiple_of(prob * HALF, HALF), HALF), :] = jnp.concatenate(
                    [blk[0:HALF], blk[HALF:CHUNK]], axis=1)
            attn = jnp.where(mask, qk_raw * decay, 0.0)
            yield

            last = (CHUNK - 1, 0)[d]
            tot = jnp.concatenate(
                [jnp.broadcast_to(csum[c * CHUNK + last:c * CHUNK + last + 1, :], (CHUNK, LANES))
                 for c in range(cps)], axis=0)
            egc = jnp.exp(csum)
            u_s[d, rows, :] = (v * beta).astype(bf16)
            w_s[d, rows, :] = (k * beta * egc).astype(bf16)
            qg_s[d, rows, :] = (q * egc).astype(bf16)
            yield
            kg_t = (k * jnp.exp(tot - csum)).T
            glast = jnp.exp(tot)
            for c in range(cps):
                cs = slice(c * CHUNK, (c + 1) * CHUNK)
                ak_s[d, st * cps + c] = jnp.concatenate([attn[cs, cs], kg_t[:, cs]], axis=0).astype(bf16)
                gl_s[d, pl.ds(st * cps + c, 1), :] = glast[c * CHUNK:c * CHUNK + 1, :]
            yield

    def back(st):
        rows = pl.ds(pl.multiple_of(st * SUPER, SUPER), SUPER)
        lane2 = lax.broadcasted_iota(jnp.int32, (CHUNK, SUPER), 1)
        for d in range(2):
            blocks = []
            for c in range(cps):
                prob = d * nch + st * cps + c
                half = a2_s[pl.ds(pl.multiple_of(prob * HALF, HALF), HALF), :]
                blk = jnp.concatenate([half[:, 0:CHUNK], half[:, CHUNK:2 * CHUNK]], axis=0)
                wide = jnp.concatenate([blk] * cps, axis=1)
                blocks.append(jnp.where(lane2 // CHUNK == c, wide, 0.0))
            t_m = jnp.concatenate(blocks, axis=0)
            if d == 1:
                t_m = t_m.T
            yield
            rhs = jnp.concatenate([u_s[d, rows, :], w_s[d, rows, :]], axis=1)
            sol = rhs.astype(f32) + _dot(t_m.astype(bf16), rhs)
            u_s[d, rows, :] = sol[:, 0:DV_A].astype(bf16)
            w_s[d, rows, :] = sol[:, DV_A:2 * DV_A].astype(bf16)
            yield

    def interleave(*gens):
        active = list(gens)
        while active:
            for g in list(active):
                if next(g, StopIteration) is StopIteration:
                    active.remove(g)

    nprob = 2 * nch
    if nprob < LANES:
        a2_s[...] = jnp.zeros_like(a2_s)

    tiles_per_body = 2 if nsuper % 2 == 0 else 1

    def front_loop(m, carry):
        interleave(*[front(m * tiles_per_body + t) for t in range(tiles_per_body)])
        return carry

    lax.fori_loop(0, nsuper // tiles_per_body, front_loop, 0)

    for r in range(HALF):
        xt = a2_s[pl.ds(r, LANES, stride=HALF), :].T
        x_s[r] = xt[0:CHUNK]
        x_s[r + HALF] = xt[CHUNK:2 * CHUNK]

    nblk = CHUNK // SUBLANES
    for i in range(1, CHUNK):
        acc = [-x_s[i, jb * SUBLANES:(jb + 1) * SUBLANES, :] for jb in range(nblk)]
        for k in range(1, i):
            a_ik = x_s[i, k:k + 1, :]
            for jb in range((k - 1) // SUBLANES + 1):
                acc[jb] = acc[jb] - a_ik * x_s[k, jb * SUBLANES:(jb + 1) * SUBLANES, :]
        for jb in range(nblk):
            x_s[i, jb * SUBLANES:(jb + 1) * SUBLANES, :] = acc[jb]

    for r in range(HALF):
        yt = jnp.concatenate([x_s[r], x_s[r + HALF]], axis=0).T
        a2_s[pl.ds(r, LANES, stride=HALF), :] = yt

    def back_loop(st, carry):
        interleave(back(st))
        return carry

    lax.fori_loop(0, nsuper, back_loop, 0)

    st_s[0] = s0f_ref[...]
    st_s[1] = s0b_ref[...]

    def scan_step(n, accumulate):
        for d in range(2):
            c = n if d == 0 else nch - 1 - n
            rows = pl.ds(pl.multiple_of(c * CHUNK, CHUNK), CHUNK)
            s_old = st_s[d]
            ws = _dot(jnp.concatenate([w_s[d, rows, :], qg_s[d, rows, :]], axis=0), s_old.astype(bf16))
            v_new = (u_s[d, rows, :].astype(f32) - ws[0:CHUNK]).astype(bf16)
            ov = _dot(ak_s[d, c], v_new)
            o = ws[CHUNK:2 * CHUNK] + ov[0:CHUNK]
            st_s[d] = s_old * gl_s[d, pl.ds(c, 1), :] + ov[CHUNK:CHUNK + DK_A]
            if accumulate:
                o_s[rows, :] = o_s[rows, :] + o
            else:
                o_s[rows, :] = o

    def scan_a(n, carry):
        scan_step(n, False)
        return carry

    def scan_b(n, carry):
        scan_step(n, True)
        return carry

    lax.fori_loop(0, nch // 2, scan_a, 0)
    lax.fori_loop(nch // 2, nch, scan_b, 0)

    def finish(st, carry):
        rows = pl.ds(pl.multiple_of(st * SUPER, SUPER), SUPER)
        o = o_s[rows, :]
        yn = o * lax.rsqrt(jnp.mean(o * o, axis=-1, keepdims=True) + EPS) * gn_ref[...]
        y_ref[rows, :] = (yn * _silu(z_ref[rows, :].astype(f32))).astype(bf16)
        return carry

    lax.fori_loop(0, nsuper, finish, 0)
    sf_ref[...] = st_s[0]
    sb_ref[...] = st_s[1]


def gdn(qkv_all, bg_all, z_all, ya_all, conv_w, gdn_g, s0f, s0b, *, batch, seq, row_block0):
    nsuper = seq // SUPER
    nch = seq // CHUNK
    nch_pad = max(nch, SUBLANES)
    cw = jnp.zeros((3 * H_A, SUBLANES, LANES), f32).at[:, :CONV_K, :].set(
        conv_w.reshape(CONV_K, 3 * H_A, LANES).transpose(1, 0, 2))

    col = lambda off: pl.BlockSpec((seq, LANES), lambda b, h: (row_block0 + b, off + h))
    cwspec = lambda off: pl.BlockSpec((None, SUBLANES, LANES), lambda b, h: (off + h, 0, 0))
    st_spec = pl.BlockSpec((None, None, DK_A, DV_A), lambda b, h: (b, h, 0, 0))
    kern = functools.partial(_gdn_kernel, seq=seq, nsuper=nsuper)
    vmem = (2 * seq * LANES * (3 * 2 + 2 + 4 + 2) + seq * LANES * (6 * 2 + 6 * 2 + 4)
            + 4 * 4 * DK_A * DV_A * 4 + 20 * 1024 * 1024)
    return pl.pallas_call(
        kern,
        out_shape=(
            jax.ShapeDtypeStruct(ya_all.shape, bf16),
            jax.ShapeDtypeStruct((batch, H_A, DK_A, DV_A), f32),
            jax.ShapeDtypeStruct((batch, H_A, DK_A, DV_A), f32),
        ),
        grid=(batch, H_A),
        in_specs=[
            col(0), col(H_A), col(2 * H_A),
            pl.BlockSpec((seq, LANES), lambda b, h: (row_block0 + b, 0)),
            col(0),
            cwspec(0), cwspec(H_A), cwspec(2 * H_A),
            pl.BlockSpec((1, DV_A), lambda b, h: (0, 0)),
            st_spec, st_spec,
            pl.BlockSpec(memory_space=pl.ANY),
        ],
        out_specs=(col(0), st_spec, st_spec),
        scratch_shapes=[
            pltpu.VMEM((2, seq, LANES), bf16),
            pltpu.VMEM((2, seq, LANES), bf16),
            pltpu.VMEM((2, seq, LANES), bf16),
            pltpu.VMEM((2, nch, CHUNK + DK_A, CHUNK), bf16),
            pltpu.VMEM((2, nch_pad, LANES), f32),
            pltpu.VMEM((seq, LANES), f32),
            pltpu.VMEM((2, DK_A, DV_A), f32),
            pltpu.VMEM((LANES * HALF, LANES), f32),
            pltpu.VMEM((CHUNK, CHUNK, LANES), f32),
        ],
        input_output_aliases={11: 0},
        compiler_params=pltpu.CompilerParams(
            dimension_semantics=("parallel", "parallel"), vmem_limit_bytes=_vmem_limit(vmem)),
        name="gdn",
    )(qkv_all, qkv_all, qkv_all, bg_all, z_all, cw, cw, cw, gdn_g.reshape(1, DV_A), s0f, s0b, ya_all)


def _attn_kernel(sink_ref, q_ref, kp_ref, kc_ref, kn_ref, kx_ref, o_ref, *, n_lat_blocks, blocks_per_seq):
    i = pl.program_id(0)
    is_lat = i < n_lat_blocks
    n = i % blocks_per_seq
    kv = jnp.concatenate([kp_ref[...], kc_ref[...], kn_ref[...], kx_ref[...]], axis=0)
    k2 = kv[:, 0:LANES]
    v2 = kv[:, LANES:2 * LANES]
    nctx = kx_ref.shape[0]

    row = lax.broadcasted_iota(jnp.int32, (BLK, BLK), 0)
    col = lax.broadcasted_iota(jnp.int32, (BLK, BLK), 1)
    ok_prev = (col >= row) & is_lat & (n > 0)
    ok_cent = jnp.broadcast_to(is_lat, (BLK, BLK))
    ok_next = (col <= row) & is_lat & (n < blocks_per_seq - 1)
    zeros = jnp.zeros((BLK, BLK), f32)
    bias = jnp.concatenate(
        [jnp.where(ok_prev, zeros, NEG_BIG), jnp.where(ok_cent, zeros, NEG_BIG),
         jnp.where(ok_next, zeros, NEG_BIG), jnp.zeros((BLK, nctx), f32)], axis=1)
    bias2 = jnp.concatenate([bias, bias], axis=0)

    lane = lax.broadcasted_iota(jnp.int32, (BLK, LANES), 1)
    lo = lane < DH_B
    lo_bf = jnp.where(lo, 1.0, 0.0).astype(bf16)
    hi_bf = jnp.where(lo, 0.0, 1.0).astype(bf16)
    rows2 = lax.broadcasted_iota(jnp.int32, (2 * BLK, 1), 0)
    for p in range(HQ_B // 2):
        q2 = q_ref[:, p * LANES:(p + 1) * LANES]
        qs = jnp.concatenate([q2 * lo_bf, q2 * hi_bf], axis=0)
        s = _dot_nt(qs, k2) + bias2
        sink = jnp.where(rows2 < BLK, sink_ref[p], sink_ref[HQ_B // 2 + p])
        m = jnp.maximum(jnp.max(s, axis=-1, keepdims=True), sink)
        e = jnp.exp(s - m)
        den = jnp.sum(e, axis=-1, keepdims=True) + jnp.exp(sink - m)
        o = _dot(e.astype(bf16), v2) * (1.0 / den)
        o_ref[:, p * LANES:(p + 1) * LANES] = jnp.where(lo, o[0:BLK], o[BLK:2 * BLK]).astype(bf16)


def attention(qb_all, kv_all, sink, *, batch, seq, ctx_len, with_ctx_queries):
    t_lat = batch * seq
    bps = seq // BLK
    n_lat = t_lat // BLK
    cps = ctx_len // BLK
    n_blocks = n_lat + (batch * cps if with_ctx_queries else 0)
    ctx_blk0 = t_lat // ctx_len

    def bidx(i):
        return jnp.where(i < n_lat, i // bps, (i - n_lat) // cps)

    def nbr(delta):
        def f(i):
            n = i % bps
            j = jnp.clip(n + delta, 0, bps - 1)
            return jnp.where(i < n_lat, (i // bps) * bps + j, 0)
        return f

    kern = functools.partial(_attn_kernel, n_lat_blocks=n_lat, blocks_per_seq=bps)
    return pl.pallas_call(
        kern,
        out_shape=jax.ShapeDtypeStruct((n_blocks * BLK, W_QB), bf16),
        grid=(n_blocks,),
        in_specs=[
            pl.BlockSpec(memory_space=pltpu.SMEM),
            pl.BlockSpec((BLK, W_QB), lambda i: (i, 0)),
            pl.BlockSpec((BLK, W_KV), lambda i: (nbr(-1)(i), 0)),
            pl.BlockSpec((BLK, W_KV), lambda i: (nbr(0)(i), 0)),
            pl.BlockSpec((BLK, W_KV), lambda i: (nbr(1)(i), 0)),
            pl.BlockSpec((ctx_len, W_KV), lambda i: (ctx_blk0 + bidx(i), 0)),
        ],
        out_specs=pl.BlockSpec((BLK, W_QB), lambda i: (i, 0)),
        compiler_params=pltpu.CompilerParams(
            dimension_semantics=("parallel",), vmem_limit_bytes=_vmem_limit(32 * 1024 * 1024)),
        name="attn",
    )(sink, qb_all, kv_all, kv_all, kv_all, kv_all)


def _merge_kernel(x_ref, ya_ref, yb_ref, gate_ref, woa_ref, wob_ref, wo_ref, g1_ref, n2_ref, sh2_ref, sc2_ref,
                  wrh_ref, wrl_ref, rb_ref, x1_ref, hp_ref, e12_ref, wk_ref):
    pa = _dot(ya_ref[...], woa_ref[...])
    pb = _dot(yb_ref[...], wob_ref[...])
    m = gate_ref[:, 0:D_MODEL].astype(f32) * pa + gate_ref[:, D_MODEL:2 * D_MODEL].astype(f32) * pb
    y = _dot(m.astype(bf16), wo_ref[...])
    x1 = x_ref[...] + g1_ref[...] * y
    x1_ref[...] = x1
    ms = jnp.mean(x1 * x1, axis=-1, keepdims=True)
    h2 = (x1 * lax.rsqrt(ms + EPS) * n2_ref[...]) * (1.0 + sc2_ref[...]) + sh2_ref[...]
    h_hi = h2.astype(bf16)
    hp_ref[...] = _pack_halves(h_hi.astype(f32))
    h_lo = (h2 - h_hi.astype(f32)).astype(bf16)

    logit = _dot_nt(wrh_ref[...], h_hi) + _dot_nt(wrl_ref[...], h_hi) + _dot_nt(wrh_ref[...], h_lo)
    score = _sigmoid(logit)
    sel = score + rb_ref[...]
    tm = sel.shape[1]

    def top2_in_group(g):
        r = [sel[g * EXP_PER_GROUP + j:g * EXP_PER_GROUP + j + 1, :] for j in range(EXP_PER_GROUP)]
        pair = None
        for a in range(EXP_PER_GROUP):
            for b in range(a + 1, EXP_PER_GROUP):
                s_ab = r[a] + r[b]
                pair = s_ab if pair is None else jnp.maximum(pair, s_ab)
        best = r[0]
        i1 = jnp.zeros((1, tm), jnp.int32)
        for j in range(1, EXP_PER_GROUP):
            better = r[j] > best
            best = jnp.where(better, r[j], best)
            i1 = jnp.where(better, j, i1)
        second = jnp.full((1, tm), -jnp.inf, f32)
        i2 = jnp.zeros((1, tm), jnp.int32)
        for j in range(EXP_PER_GROUP):
            cand = jnp.where(i1 == j, -jnp.inf, r[j])
            better = cand > second
            second = jnp.where(better, cand, second)
            i2 = jnp.where(better, j, i2)
        return pair, i1 + g * EXP_PER_GROUP, i2 + g * EXP_PER_GROUP

    gs, e1, e2 = top2_in_group(0)
    for g in range(1, N_GROUPS):
        gs_g, e1_g, e2_g = top2_in_group(g)
        better = gs_g > gs
        gs = jnp.where(better, gs_g, gs)
        e1 = jnp.where(better, e1_g, e1)
        e2 = jnp.where(better, e2_g, e2)

    eidx = lax.broadcasted_iota(jnp.int32, (N_EXPERTS, tm), 0)
    hit1 = eidx == e1
    hit2 = eidx == e2
    s1 = jnp.sum(jnp.where(hit1, score, 0.0), axis=0, keepdims=True)
    s2 = jnp.sum(jnp.where(hit2, score, 0.0), axis=0, keepdims=True)
    tot = s1 + s2
    e12_ref[...] = jnp.concatenate([e1, e2], axis=0)
    wk_pad = jnp.concatenate([s1 / tot, s2 / tot, jnp.zeros((LANES - 2, tm), f32)], axis=0)
    wk_ref[...] = wk_pad.T


def merge(x_all, ya_all, yb_all, gate_all, w_oa, w_ob, w_o, mod_l, norm2_g, wr_hi, wr_lo, router_bias,
          *, n_rows, n_lat_tiles_per_seq, n_lat_tiles, batch, tm):
    n_tiles = n_rows // tm

    def seg(i):
        return jnp.where(i < n_lat_tiles, i // n_lat_tiles_per_seq, batch)

    row = lambda w: pl.BlockSpec((tm, w), lambda i: (i, 0))
    full = lambda a, b: pl.BlockSpec((a, b), lambda i: (0, 0))
    modspec = lambda j: pl.BlockSpec((None, 1, D_MODEL), lambda i: (seg(i), 0, j))
    vmem = 2 * (2 * W_Z * D_MODEL * 2 + D_MODEL * D_MODEL * 2) + 2 * tm * (
        D_MODEL * 4 * 2 + 2 * W_Z * 2 + W_GATE * 2 + D_MODEL * 2 + LANES * 4) + 8 * tm * D_MODEL * 4
    return pl.pallas_call(
        _merge_kernel,
        out_shape=(
            jax.ShapeDtypeStruct((n_rows, D_MODEL), f32),
            jax.ShapeDtypeStruct((n_rows, D_MODEL // 2), jnp.uint32),
            jax.ShapeDtypeStruct((2, n_rows), jnp.int32),
            jax.ShapeDtypeStruct((n_rows, LANES), f32),
        ),
        grid=(n_tiles,),
        in_specs=[
            row(D_MODEL), row(W_Z), row(W_QB), row(W_GATE),
            full(W_Z, D_MODEL), full(W_QB, D_MODEL), full(D_MODEL, D_MODEL),
            modspec(2), full(1, D_MODEL), modspec(3), modspec(4),
            full(N_EXPERTS, D_MODEL), full(N_EXPERTS, D_MODEL), full(N_EXPERTS, 1),
        ],
        out_specs=(row(D_MODEL), row(D_MODEL // 2), pl.BlockSpec((2, tm), lambda i: (0, i)), row(LANES)),
        compiler_params=pltpu.CompilerParams(
            dimension_semantics=("parallel",), vmem_limit_bytes=_vmem_limit(vmem)),
        name="merge",
    )(x_all, ya_all, yb_all, gate_all, w_oa, w_ob, w_o, mod_l, norm2_g.reshape(1, D_MODEL), mod_l, mod_l,
      wr_hi, wr_lo, router_bias.reshape(N_EXPERTS, 1))


def _moe_kernel(ord_ref, tok_ref, st_ref, x1_ref, hp_ref, wk_ref, wg_ref, wu_ref, wd_ref, g2_ref, fg_ref, o_ref,
                slot_s, xg_s, yp_s, *, tm, final_norm):
    e = pl.program_id(1)
    half = D_MODEL // 2
    start = st_ref[0, e]
    count = st_ref[0, e + 1] - start

    @pl.when(e == 0)
    def _():
        xg_s[...] = jnp.zeros_like(xg_s)

    def chunk(c, carry):
        base = start + c * MOE_CHUNK
        ngroups = (jnp.minimum(count - c * MOE_CHUNK, MOE_CHUNK) + SUBLANES - 1) // SUBLANES

        def gather(g, carry2):
            r0 = pl.multiple_of(g * SUBLANES, SUBLANES)
            rows = [hp_ref[pl.ds(tok_ref[0, base + r0 + u], 1), :] for u in range(SUBLANES)]
            xg_s[pl.ds(r0, SUBLANES), :] = jnp.concatenate(rows, axis=0)
            return carry2

        lax.fori_loop(0, ngroups, gather, 0)

        def ffn(m):
            x_lo, x_hi = _unpack_halves(xg_s[0:m, :])
            x_lo = x_lo.astype(bf16)
            x_hi = x_hi.astype(bf16)
            a = _dot(x_lo, wg_ref[0:half, :]) + _dot(x_hi, wg_ref[half:D_MODEL, :])
            b = _dot(x_lo, wu_ref[0:half, :]) + _dot(x_hi, wu_ref[half:D_MODEL, :])
            he = (_silu(a) * b).astype(bf16)
            yp_s[0:m, :] = _pack_halves(_dot(he, wd_ref[...]))

        short = ngroups * SUBLANES <= MOE_CHUNK_SHORT

        @pl.when(short)
        def _():
            ffn(MOE_CHUNK_SHORT)

        @pl.when(jnp.logical_not(short))
        def _():
            ffn(MOE_CHUNK)

        def scatter(g, carry2):
            r0 = pl.multiple_of(g * SUBLANES, SUBLANES)
            blk = yp_s[pl.ds(r0, SUBLANES), :]
            for u in range(SUBLANES):
                slot_s[pl.ds(ord_ref[0, base + r0 + u], 1), :] = blk[u:u + 1, :]
            return carry2

        lax.fori_loop(0, ngroups, scatter, 0)
        return carry

    lax.fori_loop(0, (count + MOE_CHUNK - 1) // MOE_CHUNK, chunk, 0)

    @pl.when(e == N_EXPERTS - 1)
    def _():
        sub = 256
        lane = lax.broadcasted_iota(jnp.int32, (sub, LANES), 1)
        for r in range(tm // sub):
            rows = pl.ds(r * sub, sub)
            wk = wk_ref[rows, :]
            w1 = jnp.sum(jnp.where(lane == 0, wk, 0.0), axis=-1, keepdims=True)
            w2 = jnp.sum(jnp.where(lane == 1, wk, 0.0), axis=-1, keepdims=True)
            lo1, hi1 = _unpack_halves(slot_s[pl.ds(r * sub, sub), :])
            lo2, hi2 = _unpack_halves(slot_s[pl.ds(tm + r * sub, sub), :])
            acc = jnp.concatenate([w1 * lo1 + w2 * lo2, w1 * hi1 + w2 * hi2], axis=1)
            x2 = x1_ref[rows, :] + g2_ref[...] * acc
            if final_norm:
                ms = jnp.mean(x2 * x2, axis=-1, keepdims=True)
                x2 = x2 * lax.rsqrt(ms + EPS) * fg_ref[...]
            o_ref[rows, :] = x2


def moe(x1, hp, e12, wk, w_gate, w_up, w_down, mod_l, final_g, *, n_lat_tiles_per_seq, n_lat_tiles, batch, tm,
        final_norm):
    n_rows = x1.shape[0]
    n_tiles = n_rows // tm

    ids = e12.reshape(2, n_tiles, tm).transpose(1, 0, 2).reshape(n_tiles, 2 * tm)
    order = jnp.argsort(ids, axis=1, stable=True).astype(jnp.int32)
    tok = jnp.pad(jnp.where(order >= tm, order - tm, order), ((0, 0), (0, MOE_CHUNK)))
    order = jnp.pad(order, ((0, 0), (0, MOE_CHUNK)), constant_values=2 * tm)
    counts = jnp.sum((ids[:, :, None] == jnp.arange(N_EXPERTS, dtype=jnp.int32)[None, None, :]).astype(jnp.int32),
                     axis=1)
    starts = jnp.concatenate([jnp.zeros((n_tiles, 1), jnp.int32), jnp.cumsum(counts, axis=1)], axis=1)
    starts = jnp.pad(starts, ((0, 0), (0, 2 * N_EXPERTS - (N_EXPERTS + 1))))

    def seg(i):
        return jnp.where(i < n_lat_tiles, i // n_lat_tiles_per_seq, batch)

    row = lambda w: pl.BlockSpec((tm, w), lambda i, e: (i, 0))
    smem_row = lambda w: pl.BlockSpec((None, 1, w), lambda i, e: (i, 0, 0), memory_space=pltpu.SMEM)
    kern = functools.partial(_moe_kernel, tm=tm, final_norm=final_norm)
    vmem = (2 * 3 * D_MODEL * D_EXPERT * 2 + 2 * tm * (D_MODEL * 4 * 2 + D_MODEL * 2 + LANES * 4)
            + 2 * tm * D_MODEL * 2 + 16 * 1024 * 1024)
    return pl.pallas_call(
        kern,
        out_shape=jax.ShapeDtypeStruct((n_rows, D_MODEL), f32),
        grid=(n_tiles, N_EXPERTS),
        in_specs=[
            smem_row(2 * tm + MOE_CHUNK), smem_row(2 * tm + MOE_CHUNK), smem_row(2 * N_EXPERTS),
            pl.BlockSpec((tm, D_MODEL), lambda i, e: (i, 0), pipeline_mode=pl.Buffered(1)),
            row(D_MODEL // 2), row(LANES),
            pl.BlockSpec((None, D_MODEL, D_EXPERT), lambda i, e: (e, 0, 0)),
            pl.BlockSpec((None, D_MODEL, D_EXPERT), lambda i, e: (e, 0, 0)),
            pl.BlockSpec((None, D_EXPERT, D_MODEL), lambda i, e: (e, 0, 0)),
            pl.BlockSpec((None, 1, D_MODEL), lambda i, e: (seg(i), 0, 5)),
            pl.BlockSpec((1, D_MODEL), lambda i, e: (0, 0)),
        ],
        out_specs=row(D_MODEL),
        scratch_shapes=[
            pltpu.VMEM((2 * tm + SUBLANES, D_MODEL // 2), jnp.uint32),
            pltpu.VMEM((MOE_CHUNK, D_MODEL // 2), jnp.uint32),
            pltpu.VMEM((MOE_CHUNK, D_MODEL // 2), jnp.uint32),
        ],
        compiler_params=pltpu.CompilerParams(
            dimension_semantics=("parallel", "arbitrary"), vmem_limit_bytes=_vmem_limit(vmem)),
        name="moe",
    )(order.reshape(n_tiles, 1, 2 * tm + MOE_CHUNK), tok.reshape(n_tiles, 1, 2 * tm + MOE_CHUNK),
      starts.reshape(n_tiles, 1, 2 * N_EXPERTS), x1, hp, wk,
      w_gate, w_up, w_down, mod_l, final_g.reshape(1, D_MODEL))


def _prep_w_in(w_in_l):
    qkv_a = 3 * H_A * DK_A
    o_z = qkv_a
    o_b = o_z + H_A * DV_A
    o_a = o_b + 2 * H_A
    o_q = o_a + 2 * H_A
    o_k = o_q + HQ_B * DH_B
    o_v = o_k + HKV_B * DH_B
    o_ga = o_v + HKV_B * DH_B
    w_bg = jnp.zeros((D_MODEL, W_BG), f32).at[:, :4 * H_A].set(w_in_l[:, o_b:o_q])
    wq = w_in_l[:, o_q:o_k].reshape(D_MODEL, HQ_B, DH_B)[:, np.array(HEAD_PERM), :].reshape(D_MODEL, W_QB)
    wq = wq * (DH_B ** -0.5)
    cat = jnp.concatenate(
        [w_in_l[:, :qkv_a], w_in_l[:, o_z:o_b], w_bg, wq, w_in_l[:, o_k:o_v], w_in_l[:, o_v:o_ga],
         w_in_l[:, o_ga:]], axis=1)
    return cat.astype(bf16)


def _rope_tables(seq, tm):
    t = np.arange(seq)
    rows = (t // GRID_W).astype(np.float32)
    cols = (t % GRID_W).astype(np.float32)
    nf = DH_B // 4
    inv = jnp.asarray(ROPE_THETA, f32) ** (-jnp.arange(nf, dtype=f32) / nf)
    ang_r = jnp.asarray(rows)[:, None] * inv[None, :]
    ang_c = jnp.asarray(cols)[:, None] * inv[None, :]
    cos64 = jnp.concatenate([jnp.cos(ang_r), jnp.cos(ang_r), jnp.cos(ang_c), jnp.cos(ang_c)], axis=1)
    zero = jnp.zeros_like(ang_r)
    sa64 = jnp.concatenate([-jnp.sin(ang_r), zero, -jnp.sin(ang_c), zero], axis=1)
    sb64 = jnp.concatenate([zero, jnp.sin(ang_r), zero, jnp.sin(ang_c)], axis=1)

    def fin(tab, fill):
        tab = jnp.concatenate([tab, tab], axis=1)
        return jnp.concatenate([tab, jnp.full((tm, LANES), fill, f32)], axis=0)

    return fin(cos64, 1.0), fin(sa64, 0.0), fin(sb64, 0.0)


TM_TOKEN = 256
TM_MERGE = 512
TM_MOE = 2048


def kernel(x, c, ctx, c_ctx, w_mod, b_mod, norm1_g, norm2_g, w_in, conv_w, a_log, dt_bias, gdn_norm_g, sink,
           w_oa, w_ob, w_o, w_router, router_bias, w_gate, w_up, w_down, final_g):
    batch, seq, d = x.shape
    ctx_len = ctx.shape[1]
    depth = w_mod.shape[0]
    assert d == D_MODEL and seq % SUPER == 0 and ctx_len % SUPER == 0 and seq % TM_MOE == 0
    t_lat = batch * seq
    t_ctx = batch * ctx_len
    assert t_ctx % TM_MOE == 0

    x_all = jnp.concatenate([x.reshape(t_lat, d), ctx.reshape(t_ctx, d)], axis=0)
    t_all = t_lat + t_ctx

    n_cond = ((batch + 1 + SUBLANES - 1) // SUBLANES) * SUBLANES
    cond = jnp.zeros((n_cond, d), f32).at[:batch].set(c).at[batch].set(c_ctx)
    mod = adaln_all(cond, w_mod, b_mod).reshape(depth, n_cond, 1, 6 * d)

    rope_tabs = _rope_tables(seq, TM_TOKEN)
    wr_hi = w_router.T.astype(bf16)
    wr_lo = (w_router.T - wr_hi.astype(f32)).astype(bf16)
    perm_rows = np.concatenate([np.arange(h * DH_B, (h + 1) * DH_B) for h in HEAD_PERM])
    zero_state = jnp.zeros((batch, H_A, DK_A, DV_A), f32)

    tok = dict(n_lat_tiles_per_seq=seq // TM_TOKEN, n_lat_tiles=t_lat // TM_TOKEN, batch=batch, tm=TM_TOKEN)
    tok_merge = dict(n_lat_tiles_per_seq=seq // TM_MERGE, n_lat_tiles=t_lat // TM_MERGE, batch=batch, tm=TM_MERGE)
    tok_moe = dict(n_lat_tiles_per_seq=seq // TM_MOE, n_lat_tiles=t_lat // TM_MOE, batch=batch, tm=TM_MOE)

    for l in range(depth):
        need_ctx = l < depth - 1
        w_cat = _prep_w_in(w_in[l])
        qkv_all, z_all, bg_all, qb_all, kv_all, gate_all = inproj(
            x_all, mod[l], norm1_g[l], w_cat, rope_tabs, a_log[l], dt_bias[l], **tok)

        ya_all = jnp.zeros((t_all, W_Z), bf16)
        ya_all, s_f, s_b = gdn(qkv_all, bg_all, z_all, ya_all, conv_w[l], gdn_norm_g[l], zero_state, zero_state,
                               batch=batch, seq=ctx_len, row_block0=t_lat // ctx_len)
        ya_all, _, _ = gdn(qkv_all, bg_all, z_all, ya_all, conv_w[l], gdn_norm_g[l], s_f, s_b,
                           batch=batch, seq=seq, row_block0=0)

        yb_all = attention(qb_all, kv_all, sink[l], batch=batch, seq=seq, ctx_len=ctx_len,
                           with_ctx_queries=need_ctx)

        n_rows = t_all if need_ctx else t_lat
        x1, hp, e12, wk = merge(
            x_all, ya_all, yb_all, gate_all, w_oa[l].astype(bf16), w_ob[l][perm_rows].astype(bf16),
            w_o[l].astype(bf16), mod[l], norm2_g[l], wr_hi, wr_lo, router_bias, n_rows=n_rows, **tok_merge)
        x_all = moe(x1, hp, e12, wk, w_gate[l].astype(bf16), w_up[l].astype(bf16), w_down[l].astype(bf16),
                    mod[l], final_g, final_norm=not need_ctx, **tok_moe)

    return x_all[:t_lat].reshape(batch, seq, d)
```
